```python
import numpy as np
import jax
import jax.numpy as jnp
from jax import lax

D_MODEL = 2048
BATCH = 4
SEQ = 2048
DEPTH = 2

N_MIXERS = 4
HEAD_DIM = 128
GROUP_WIDTH = D_MODEL // N_MIXERS
MIX_WIDTH = N_MIXERS * GROUP_WIDTH
N_HEADS = GROUP_WIDTH // HEAD_DIM
Q_BLOCK = 128
ROPE_THETA = 10000.0
RMS_EPS = 1e-6
NEG_INF = -1e30

CMP_LEN = 32
CMP_STRIDE = 16
CMP_HIDDEN = HEAD_DIM
SEL_LEN = 64
SEL_TOPN = 16
WINDOW = 512
FORCED_BONUS = 1e6

MLA_Q_RANK = 384
MLA_KV_RANK = 128
MLA_NOPE = 128
MLA_ROPE = 64
MLA_V = 128

IN_SPLITS = (
    ("sb_q", GROUP_WIDTH), ("sb_k", GROUP_WIDTH), ("sb_v", GROUP_WIDTH), ("sb_gate", GROUP_WIDTH),
    ("nsa_q", GROUP_WIDTH), ("nsa_k_cmp", HEAD_DIM), ("nsa_v_cmp", HEAD_DIM),
    ("nsa_k_sel", HEAD_DIM), ("nsa_v_sel", HEAD_DIM), ("nsa_k_win", HEAD_DIM), ("nsa_v_win", HEAD_DIM),
    ("nsa_branch", 3 * N_HEADS), ("nsa_gate", GROUP_WIDTH),
    ("fox_q", GROUP_WIDTH), ("fox_k", GROUP_WIDTH), ("fox_v", GROUP_WIDTH), ("fox_f", N_HEADS),
    ("fox_gate", GROUP_WIDTH),
    ("mla_cq", MLA_Q_RANK), ("mla_ckv", MLA_KV_RANK), ("mla_k_rope", MLA_ROPE), ("mla_gate", GROUP_WIDTH),
)
IN_WIDTH = sum(width for _, width in IN_SPLITS)

kernel_name = "hybrid_sb_nsa_fox_mla_layer"


def rms_norm(x, g):
    xf = x.astype(jnp.float32)
    y = xf * lax.rsqrt(jnp.mean(xf * xf, axis=-1, keepdims=True) + RMS_EPS)
    return (y * g.astype(jnp.float32)).astype(x.dtype)


def apply_rope(x, pos):
    half = x.shape[-1] // 2
    inv_freq = ROPE_THETA ** (-jnp.arange(half, dtype=jnp.float32) / half)
    ang = pos.astype(jnp.float32)[:, None] * inv_freq[None, :]
    cos = jnp.cos(ang)[:, None, :]
    sin = jnp.sin(ang)[:, None, :]
    xf = x.astype(jnp.float32)
    x1, x2 = xf[..., :half], xf[..., half:]
    return jnp.concatenate([x1 * cos - x2 * sin, x2 * cos + x1 * sin], axis=-1).astype(x.dtype)


def masked_softmax(z, mask):
    p = jax.nn.softmax(jnp.where(mask, z, NEG_INF), axis=-1)
    return jnp.where(mask, p, 0.0)


def sweep_query_blocks(block_fn, seq_len):
    out = lax.map(block_fn, jnp.arange(seq_len // Q_BLOCK))
    n_blocks, b, q, h, d = out.shape
    return jnp.moveaxis(out, 0, 1).reshape(b, n_blocks * q, h, d)


def split_columns(z):
    parts = {}
    offset = 0
    for name, width in IN_SPLITS:
        parts[name] = z[..., offset:offset + width]
        offset += width
    return parts


def stick_breaking_attention(q, k, v):
    B, S, H, d = q.shape
    scale = d ** -0.5
    kpos = jnp.arange(S)

    def block(i):
        s0 = i * Q_BLOCK
        qb = lax.dynamic_slice_in_dim(q, s0, Q_BLOCK, axis=1)
        qpos = s0 + jnp.arange(Q_BLOCK)
        z = jnp.einsum("bqhd,bshd->bhqs", qb, k).astype(jnp.float32) * scale
        earlier = kpos[None, :] < qpos[:, None]
        log_keep = jnp.where(earlier, jax.nn.log_sigmoid(-z), 0.0)
        log_after = lax.cumsum(log_keep, axis=3, reverse=True) - log_keep
        w = jnp.where(earlier, jnp.exp(jax.nn.log_sigmoid(z) + log_after), 0.0)
        return jnp.einsum("bhqs,bshd->bqhd", w.astype(v.dtype), v)

    return sweep_query_blocks(block, S)


def compress_blocks(tok, pos_emb, w1, w2):
    B, S, d = tok.shape
    n_cmp = (S - CMP_LEN) // CMP_STRIDE + 1
    gather = np.arange(n_cmp)[:, None] * CMP_STRIDE + np.arange(CMP_LEN)[None, :]
    blocks = tok[:, gather] + pos_emb
    hidden = jax.nn.silu(blocks.reshape(B, n_cmp, CMP_LEN * d) @ w1)
    return hidden @ w2


def native_sparse_attention(q, kc_tok, vc_tok, ks, vs, kw, vw, branch_gates,
                            pos_k, w1_k, w2_k, pos_v, w1_v, w2_v, pos):
    B, S, H, d = q.shape
    scale = d ** -0.5
    n_cmp = (S - CMP_LEN) // CMP_STRIDE + 1
    n_sel = S // SEL_LEN
    top_n = min(SEL_TOPN, n_sel)
    cmp_start = np.arange(n_cmp) * CMP_STRIDE
    cmp_end = jnp.asarray(cmp_start + CMP_LEN - 1, dtype=jnp.int32)
    sel_start = np.arange(n_sel) * SEL_LEN
    overlap = np.clip(np.minimum(cmp_start[:, None] + CMP_LEN, sel_start[None, :] + SEL_LEN)
                      - np.maximum(cmp_start[:, None], sel_start[None, :]), 0, None)
    cmp_to_sel = jnp.asarray(overlap / CMP_LEN, dtype=jnp.float32)

    q = apply_rope(q, pos)
    kc = compress_blocks(kc_tok, pos_k, w1_k, w2_k)
    kc = apply_rope(kc[:, :, None, :], cmp_end)[:, :, 0, :]
    vc = compress_blocks(vc_tok, pos_v, w1_v, w2_v)
    ks = apply_rope(ks[:, :, None, :], pos)[:, :, 0, :]
    kw = apply_rope(kw[:, :, None, :], pos)[:, :, 0, :]
    ks_blocks = ks.reshape(B, n_sel, SEL_LEN, d)
    vs_blocks = vs.reshape(B, n_sel, SEL_LEN, d)
    kw_pad = jnp.pad(kw, ((0, 0), (WINDOW, 0), (0, 0)))
    vw_pad = jnp.pad(vw, ((0, 0), (WINDOW, 0), (0, 0)))
    sel_ids = jnp.arange(n_sel)
    gather_blocks = jax.vmap(lambda blocks, ids: blocks[ids])

    def block(i):
        s0 = i * Q_BLOCK
        qb = lax.dynamic_slice_in_dim(q, s0, Q_BLOCK, axis=1)
        gb = lax.dynamic_slice_in_dim(branch_gates, s0, Q_BLOCK, axis=1)
        qpos = s0 + jnp.arange(Q_BLOCK)
        zc = jnp.einsum("bqhd,bnd->bhqn", qb, kc).astype(jnp.float32) * scale
        pc = masked_softmax(zc, cmp_end[None, :] <= qpos[:, None])
        o_cmp = jnp.einsum("bhqn,bnd->bqhd", pc.astype(vc.dtype), vc)
        imp = jnp.einsum("bhqn,ns->bqs", pc, cmp_to_sel)
        cur = qpos // SEL_LEN
        valid = sel_ids[None, :] <= cur[:, None]
        forced = ((sel_ids[None, :] == 0) | (sel_ids[None, :] == cur[:, None])
                  | (sel_ids[None, :] == cur[:, None] - 1))
        score = jnp.where(valid, jnp.where(forced, FORCED_BONUS, imp), NEG_INF)
        _, idx = lax.top_k(score, top_n)
        kg = gather_blocks(ks_blocks, idx)
        vg = gather_blocks(vs_blocks, idx).reshape(B, Q_BLOCK, top_n * SEL_LEN, d)
        tok = idx[..., None] * SEL_LEN + jnp.arange(SEL_LEN)
        sel_mask = (tok <= qpos[None, :, None, None]).reshape(B, 1, Q_BLOCK, top_n * SEL_LEN)
        zs = jnp.einsum("bqhd,bqnld->bhqnl", qb, kg).astype(jnp.float32)
        zs = zs.reshape(B, H, Q_BLOCK, top_n * SEL_LEN) * scale
        ps = masked_softmax(zs, sel_mask)
        o_slc = jnp.einsum("bhqm,bqmd->bqhd", ps.astype(vg.dtype), vg)
        kwb = lax.dynamic_slice_in_dim(kw_pad, s0, WINDOW + Q_BLOCK, axis=1)
        vwb = lax.dynamic_slice_in_dim(vw_pad, s0, WINDOW + Q_BLOCK, axis=1)
        kpos = s0 - WINDOW + jnp.arange(WINDOW + Q_BLOCK)
        win_mask = ((kpos[None, :] >= 0) & (kpos[None, :] <= qpos[:, None])
                    & (kpos[None, :] > qpos[:, None] - WINDOW))
        zw = jnp.einsum("bqhd,bkd->bhqk", qb, kwb).astype(jnp.float32) * scale
        pw = masked_softmax(zw, win_mask)
        o_win = jnp.einsum("bhqk,bkd->bqhd", pw.astype(vwb.dtype), vwb)
        return gb[..., 0:1] * o_cmp + gb[..., 1:2] * o_slc + gb[..., 2:3] * o_win

    return sweep_query_blocks(block, S)


def forgetting_attention(q, k, v, log_f):
    B, S, H, d = q.shape
    scale = d ** -0.5
    cum = jnp.cumsum(log_f, axis=1).transpose(0, 2, 1)
    kpos = jnp.arange(S)

    def block(i):
        s0 = i * Q_BLOCK
        qb = lax.dynamic_slice_in_dim(q, s0, Q_BLOCK, axis=1)
        cq = lax.dynamic_slice_in_dim(cum, s0, Q_BLOCK, axis=2)
        qpos = s0 + jnp.arange(Q_BLOCK)
        z = (jnp.einsum("bqhd,bshd->bhqs", qb, k).astype(jnp.float32) * scale
             + cq[..., None] - cum[:, :, None, :])
        p = masked_softmax(z, kpos[None, :] <= qpos[:, None])
        return jnp.einsum("bhqs,bshd->bqhd", p.astype(v.dtype), v)

    return sweep_query_blocks(block, S)


def latent_attention(c_q, c_kv, k_rope, q_norm_g, w_uq, kv_norm_g, w_ukv, pos):
    B, S, _ = c_q.shape
    q = (rms_norm(c_q, q_norm_g) @ w_uq).reshape(B, S, N_HEADS, MLA_NOPE + MLA_ROPE)
    q_nope = q[..., :MLA_NOPE]
    q_rot = apply_rope(q[..., MLA_NOPE:], pos)
    kv = (rms_norm(c_kv, kv_norm_g) @ w_ukv).reshape(B, S, N_HEADS, MLA_NOPE + MLA_V)
    k_nope, v = kv[..., :MLA_NOPE], kv[..., MLA_NOPE:]
    k_rot = apply_rope(k_rope[:, :, None, :], pos)[:, :, 0, :]
    scale = (MLA_NOPE + MLA_ROPE) ** -0.5
    kpos = jnp.arange(S)

    def block(i):
        s0 = i * Q_BLOCK
        qn = lax.dynamic_slice_in_dim(q_nope, s0, Q_BLOCK, axis=1)
        qr = lax.dynamic_slice_in_dim(q_rot, s0, Q_BLOCK, axis=1)
        qpos = s0 + jnp.arange(Q_BLOCK)
        z = (jnp.einsum("bqhd,bshd->bhqs", qn, k_nope)
             + jnp.einsum("bqhr,bsr->bhqs", qr, k_rot)).astype(jnp.float32) * scale
        p = masked_softmax(z, kpos[None, :] <= qpos[:, None])
        return jnp.einsum("bhqs,bshd->bqhd", p.astype(v.dtype), v)

    return sweep_query_blocks(block, S)


def hybrid_layer(x, pre_g, post_g, w_in, b_in, w_out, forget_bias,
                 pos_k, w1_k, w2_k, pos_v, w1_v, w2_v,
                 q_norm_g, w_uq, kv_norm_g, w_ukv):
    B, S, _ = x.shape
    pos = jnp.arange(S)
    h = rms_norm(x, pre_g)
    p = split_columns(h @ w_in + b_in)

    def heads(t):
        return t.reshape(B, S, N_HEADS, HEAD_DIM)

    o_sb = stick_breaking_attention(heads(p["sb_q"]), heads(p["sb_k"]), heads(p["sb_v"]))

    branch_gates = jax.nn.sigmoid(p["nsa_branch"].reshape(B, S, N_HEADS, 3))
    o_nsa = native_sparse_attention(heads(p["nsa_q"]), p["nsa_k_cmp"], p["nsa_v_cmp"],
                                    p["nsa_k_sel"], p["nsa_v_sel"], p["nsa_k_win"], p["nsa_v_win"],
                                    branch_gates, pos_k, w1_k, w2_k, pos_v, w1_v, w2_v, pos)

    log_f = jax.nn.log_sigmoid((p["fox_f"] + forget_bias).astype(jnp.float32))
    o_fox = forgetting_attention(heads(p["fox_q"]), heads(p["fox_k"]), heads(p["fox_v"]), log_f)

    o_mla = latent_attention(p["mla_cq"], p["mla_ckv"], p["mla_k_rope"],
                             q_norm_g, w_uq, kv_norm_g, w_ukv, pos)

    mix = jnp.concatenate([
        o_sb.reshape(B, S, GROUP_WIDTH) * jax.nn.silu(p["sb_gate"]),
        o_nsa.reshape(B, S, GROUP_WIDTH) * jax.nn.silu(p["nsa_gate"]),
        o_fox.reshape(B, S, GROUP_WIDTH) * jax.nn.silu(p["fox_gate"]),
        o_mla.reshape(B, S, GROUP_WIDTH) * jax.nn.silu(p["mla_gate"]),
    ], axis=-1)
    return x + rms_norm(mix @ w_out, post_g)


def setup_inputs(seed: int = 0) -> dict:
    key = jax.random.key(seed)
    ks = jax.random.split(key, 17)
    f32 = jnp.float32

    def normal(k, shape, scale):
        return jax.random.normal(k, shape, f32) * scale

    def gain(k, shape):
        return 1.0 + 0.02 * jax.random.normal(k, shape, f32)

    flat = CMP_LEN * HEAD_DIM
    return {
        "x": normal(ks[0], (BATCH, SEQ, D_MODEL), 1.0),
        "pre_norm_g": gain(ks[1], (DEPTH, D_MODEL)),
        "post_norm_g": gain(ks[2], (DEPTH, D_MODEL)),
        "w_in": normal(ks[3], (DEPTH, D_MODEL, IN_WIDTH), D_MODEL ** -0.5),
        "b_in": normal(ks[4], (DEPTH, IN_WIDTH), 0.02),
        "w_out": normal(ks[5], (DEPTH, MIX_WIDTH, D_MODEL), MIX_WIDTH ** -0.5),
        "fox_forget_bias": jax.random.uniform(ks[6], (DEPTH, N_HEADS), f32, 1.0, 4.0),
        "nsa_cmp_pos_k": normal(ks[7], (DEPTH, CMP_LEN, HEAD_DIM), 0.02),
        "nsa_cmp_w1_k": normal(ks[8], (DEPTH, flat, CMP_HIDDEN), flat ** -0.5),
        "nsa_cmp_w2_k": normal(ks[9], (DEPTH, CMP_HIDDEN, HEAD_DIM), CMP_HIDDEN ** -0.5),
        "nsa_cmp_pos_v": normal(ks[10], (DEPTH, CMP_LEN, HEAD_DIM), 0.02),
        "nsa_cmp_w1_v": normal(ks[11], (DEPTH, flat, CMP_HIDDEN), flat ** -0.5),
        "nsa_cmp_w2_v": normal(ks[12], (DEPTH, CMP_HIDDEN, HEAD_DIM), CMP_HIDDEN ** -0.5),
        "mla_q_norm_g": gain(ks[13], (DEPTH, MLA_Q_RANK)),
        "mla_w_uq": normal(ks[14], (DEPTH, MLA_Q_RANK, N_HEADS * (MLA_NOPE + MLA_ROPE)), MLA_Q_RANK ** -0.5),
        "mla_kv_norm_g": gain(ks[15], (DEPTH, MLA_KV_RANK)),
        "mla_w_ukv": normal(ks[16], (DEPTH, MLA_KV_RANK, N_HEADS * (MLA_NOPE + MLA_V)), MLA_KV_RANK ** -0.5),
    }


def reference(x, pre_norm_g, post_norm_g, w_in, b_in, w_out, fox_forget_bias,
              nsa_cmp_pos_k, nsa_cmp_w1_k, nsa_cmp_w2_k,
              nsa_cmp_pos_v, nsa_cmp_w1_v, nsa_cmp_w2_v,
              mla_q_norm_g, mla_w_uq, mla_kv_norm_g, mla_w_ukv):
    for l in range(DEPTH):
        x = hybrid_layer(x, pre_norm_g[l], post_norm_g[l], w_in[l], b_in[l], w_out[l],
                         fox_forget_bias[l],
                         nsa_cmp_pos_k[l], nsa_cmp_w1_k[l], nsa_cmp_w2_k[l],
                         nsa_cmp_pos_v[l], nsa_cmp_w1_v[l], nsa_cmp_w2_v[l],
                         mla_q_norm_g[l], mla_w_uq[l], mla_kv_norm_g[l], mla_w_ukv[l])
    return x
```

```python
import functools

import numpy as np
import jax
import jax.numpy as jnp
from jax import lax
from jax.experimental import pallas as pl
from jax.experimental.pallas import tpu as pltpu

F32 = jnp.float32
BF16 = jnp.bfloat16

LANES = 128
HEAD_DIM = 128
N_HEADS = 4
GROUP = N_HEADS * HEAD_DIM
RMS_EPS = 1e-6
NEG_INF = -1e30
ROPE_THETA = 10000.0

CMP_LEN = 32
CMP_STRIDE = 16
SEL_LEN = 64
SEL_TOPN = 16
WINDOW = 512
FORCED_BONUS = 1e6

MLA_Q_RANK = 384
MLA_KV_RANK = 128
MLA_NOPE = 128
MLA_ROPE = 64

VMEM_LIMIT = 56 * 1024 * 1024

C_SB_Q, C_SB_K, C_SB_V, C_SB_G = 0, 512, 1024, 1536
C_NSA_Q = 2048
C_NSA_KC, C_NSA_VC, C_NSA_KS, C_NSA_VS, C_NSA_KW, C_NSA_VW = 2560, 2688, 2816, 2944, 3072, 3200
C_MISC = 3328
C_MLA_CKV = 3456
C_NSA_G = 3584
C_FOX_Q, C_FOX_K, C_FOX_V, C_FOX_G = 4096, 4608, 5120, 5632
C_MLA_CQ = 6144
C_MLA_KR = 6528
C_MLA_G = 6656
P_WIDTH = 7168
assert C_NSA_Q % GROUP == 0 and C_NSA_G % GROUP == 0 and C_MLA_CQ % MLA_Q_RANK == 0
MISC_FOX = 0
MISC_BRANCH = 8

_ORIG = {}
_off = 0
for _name, _w in (("sb_q", 512), ("sb_k", 512), ("sb_v", 512), ("sb_gate", 512),
                  ("nsa_q", 512), ("nsa_k_cmp", 128), ("nsa_v_cmp", 128), ("nsa_k_sel", 128),
                  ("nsa_v_sel", 128), ("nsa_k_win", 128), ("nsa_v_win", 128), ("nsa_branch", 12),
                  ("nsa_gate", 512), ("fox_q", 512), ("fox_k", 512), ("fox_v", 512), ("fox_f", 4),
                  ("fox_gate", 512), ("mla_cq", 384), ("mla_ckv", 128), ("mla_k_rope", 64),
                  ("mla_gate", 512)):
    _ORIG[_name] = (_off, _w)
    _off += _w
IN_WIDTH = _off


def _params(*sem):
    return pltpu.CompilerParams(dimension_semantics=sem, vmem_limit_bytes=VMEM_LIMIT)


def _dot(a, b):
    return jnp.dot(a, b, preferred_element_type=F32)


def _dot_nt(a, b):
    return lax.dot_general(a, b, (((1,), (1,)), ((), ())), preferred_element_type=F32)


def _split2(x):
    hi = x.astype(BF16)
    lo = (x - hi.astype(F32)).astype(BF16)
    return hi, lo


def _split3(x):
    hi = x.astype(BF16)
    r = x - hi.astype(F32)
    mid = r.astype(BF16)
    lo = (r - mid.astype(F32)).astype(BF16)
    return hi, mid, lo


def _rope(x, cos, sin_signed):
    return x * cos + pltpu.roll(x, 64, 1) * sin_signed


def _silu(g):
    return g / (1.0 + jnp.exp(-g))


def _sigmoid(g):
    return 1.0 / (1.0 + jnp.exp(-g))


def _softmax_step(z, v, m, l, acc):
    m_new = jnp.maximum(m, jnp.max(z, axis=1, keepdims=True))
    alpha = jnp.exp(m - m_new)
    p = jnp.exp(z - m_new)
    l = alpha * l + jnp.sum(p, axis=1, keepdims=True)
    acc = alpha * acc + _dot(p.astype(BF16), v)
    return m_new, l, acc


def _norm_matmul_kernel(x_ref, g_ref, w_ref, b_ref, o_ref, h_ref, *, chunk):
    tm = x_ref.shape[0]

    @pl.when(pl.program_id(1) == 0)
    def _():
        def body(c, _):
            rows = pl.ds(pl.multiple_of(c * chunk, chunk), chunk)
            x = x_ref[rows, :]
            ms = jnp.mean(x * x, axis=-1, keepdims=True)
            h_ref[rows, :] = (x * lax.rsqrt(ms + RMS_EPS) * g_ref[...]).astype(BF16)
            return 0
        lax.fori_loop(0, tm // chunk, body, 0)

    o_ref[...] = (_dot(h_ref[...], w_ref[...]) + b_ref[...]).astype(o_ref.dtype)


def norm_matmul(x, xcol, kdim, g, w, b, *, tm, tn, out_dtype):
    m = x.shape[0]
    n = w.shape[1]
    chunk = min(tm, 256)
    return pl.pallas_call(
        functools.partial(_norm_matmul_kernel, chunk=chunk),
        grid=(m // tm, n // tn),
        in_specs=[
            pl.BlockSpec((tm, kdim), lambda i, j: (i, xcol)),
            pl.BlockSpec((1, kdim), lambda i, j: (0, 0)),
            pl.BlockSpec((kdim, tn), lambda i, j: (0, j)),
            pl.BlockSpec((1, tn), lambda i, j: (0, j)),
        ],
        out_specs=pl.BlockSpec((tm, tn), lambda i, j: (i, j)),
        out_shape=jax.ShapeDtypeStruct((m, n), out_dtype),
        scratch_shapes=[pltpu.VMEM((tm, kdim), BF16)],
        compiler_params=_params("parallel", "arbitrary"),
        name="norm_matmul",
    )(x, g, w, b)


def _sb_kernel(q_ref, k_ref, v_ref, g_ref, u_ref, o_ref, kb_ref, vb_ref, *, tq, tk, scale):
    i = pl.program_id(2)

    @pl.when(i == 0)
    def _():
        kb_ref[...] = k_ref[...].astype(BF16)
        vb_ref[...] = v_ref[...].astype(BF16)

    q = (q_ref[...] * scale).astype(BF16)
    u = u_ref[...]
    ndiag = tq // tk
    qpos = i * tq + lax.broadcasted_iota(jnp.int32, (tq, tk), 0)
    col = lax.broadcasted_iota(jnp.int32, (tq, tk), 1)

    def block(j, carry, acc, masked):
        off = pl.multiple_of(j * tk, tk)
        kj = kb_ref[pl.ds(off, tk), :]
        vj = vb_ref[pl.ds(off, tk), :]
        z = _dot_nt(q, kj)
        softplus = jnp.maximum(z, 0.0) + jnp.log(1.0 + jnp.exp(-jnp.abs(z)))
        log_keep = -softplus
        log_beta = z - softplus
        if masked:
            mask = (off + col) < qpos
            log_keep = jnp.where(mask, log_keep, 0.0)
        hi, lo = _split2(log_keep)
        r = _dot(jnp.concatenate([hi, lo], axis=0), u)
        log_after = r[:tq] + r[tq:]
        w = jnp.exp(log_beta + log_after + carry)
        if masked:
            w = jnp.where(mask, w, 0.0)
        acc = acc + _dot(w.astype(BF16), vj)
        carry = carry + jnp.sum(log_keep, axis=1, keepdims=True)
        return carry, acc

    carry = jnp.zeros((tq, 1), F32)
    acc = jnp.zeros((tq, HEAD_DIM), F32)
    for d in range(ndiag):
        carry, acc = block(i * ndiag + (ndiag - 1 - d), carry, acc, True)

    def body(t, c):
        return block(i * ndiag - 1 - t, c[0], c[1], False)

    carry, acc = lax.fori_loop(0, i * ndiag, body, (carry, acc))
    o_ref[...] = (acc * _silu(g_ref[...])).astype(o_ref.dtype)


def sb_attention(p, batch, seq, *, tq=256, tk=128):
    nq = seq // tq
    hb = HEAD_DIM
    u = jnp.asarray(np.triu(np.ones((tk, tk), np.float32), 0).T - np.eye(tk, dtype=np.float32), BF16)
    return pl.pallas_call(
        functools.partial(_sb_kernel, tq=tq, tk=tk, scale=HEAD_DIM ** -0.5),
        grid=(batch, N_HEADS, nq),
        in_specs=[
            pl.BlockSpec((tq, hb), lambda b, h, i: (b * nq + i, C_SB_Q // hb + h)),
            pl.BlockSpec((seq, hb), lambda b, h, i: (b, C_SB_K // hb + h)),
            pl.BlockSpec((seq, hb), lambda b, h, i: (b, C_SB_V // hb + h)),
            pl.BlockSpec((tq, hb), lambda b, h, i: (b * nq + i, C_SB_G // hb + h)),
            pl.BlockSpec((tk, tk), lambda b, h, i: (0, 0)),
        ],
        out_specs=pl.BlockSpec((tq, hb), lambda b, h, i: (b * nq + i, h)),
        out_shape=jax.ShapeDtypeStruct((batch * seq, GROUP), BF16),
        scratch_shapes=[pltpu.VMEM((seq, hb), BF16), pltpu.VMEM((seq, hb), BF16)],
        compiler_params=_params("parallel", "parallel", "arbitrary"),
        name="sb_attention",
    )(p, p, p, p, u)


def _fox_prep_kernel(f_ref, bias_ref, col_ref, row_ref, *, chunk):
    seq = f_ref.shape[0]
    r = lax.broadcasted_iota(jnp.int32, (chunk, chunk), 0)
    c = lax.broadcasted_iota(jnp.int32, (chunk, chunk), 1)
    tri = jnp.where(c <= r, 1.0, 0.0).astype(BF16)
    carry = jnp.zeros((1, LANES), F32)
    for n in range(seq // chunk):
        x = f_ref[n * chunk:(n + 1) * chunk, :] + bias_ref[...]
        logf = jnp.minimum(x, 0.0) - jnp.log(1.0 + jnp.exp(-jnp.abs(x)))
        hi, mid, lo = _split3(logf)
        within = _dot(tri, hi) + _dot(tri, mid) + _dot(tri, lo)
        cum = within + carry
        col_ref[n * chunk:(n + 1) * chunk, :] = cum
        row_ref[:, n * chunk:(n + 1) * chunk] = cum.T[0:8, :]
        carry = cum[chunk - 1:chunk, :]


def fox_prep(p, bias_row, batch, seq):
    chunk = LANES
    return pl.pallas_call(
        functools.partial(_fox_prep_kernel, chunk=chunk),
        grid=(batch,),
        in_specs=[
            pl.BlockSpec((seq, LANES), lambda b: (b, C_MISC // LANES)),
            pl.BlockSpec((1, LANES), lambda b: (0, 0)),
        ],
        out_specs=[
            pl.BlockSpec((seq, LANES), lambda b: (b, 0)),
            pl.BlockSpec((None, 8, seq), lambda b: (b, 0, 0)),
        ],
        out_shape=[jax.ShapeDtypeStruct((batch * seq, LANES), F32),
                   jax.ShapeDtypeStruct((batch, 8, seq), F32)],
        compiler_params=_params("parallel"),
        name="fox_prep",
    )(p, bias_row)


def _fox_kernel(q_ref, k_ref, v_ref, g_ref, ccol_ref, crow_ref, o_ref, kb_ref, vb_ref, *, tq, tk, scale):
    h = pl.program_id(1)
    i = pl.program_id(2)

    @pl.when(i == 0)
    def _():
        kb_ref[...] = k_ref[...].astype(BF16)
        vb_ref[...] = v_ref[...].astype(BF16)

    q = (q_ref[...] * scale).astype(BF16)
    lane = lax.broadcasted_iota(jnp.int32, (tq, LANES), 1)
    cq = jnp.sum(jnp.where(lane == h, ccol_ref[...], 0.0), axis=1, keepdims=True)
    qpos = i * tq + lax.broadcasted_iota(jnp.int32, (tq, tk), 0)
    col = lax.broadcasted_iota(jnp.int32, (tq, tk), 1)

    def block(j, m, l, acc, masked):
        off = pl.multiple_of(j * tk, tk)
        kj = kb_ref[pl.ds(off, tk), :]
        vj = vb_ref[pl.ds(off, tk), :]
        ck = crow_ref[pl.ds(h, 1), pl.ds(off, tk)]
        z = _dot_nt(q, kj) + cq - ck
        if masked:
            z = jnp.where((off + col) <= qpos, z, NEG_INF)
        return _softmax_step(z, vj, m, l, acc)

    m = jnp.full((tq, 1), NEG_INF, F32)
    l = jnp.zeros((tq, 1), F32)
    acc = jnp.zeros((tq, HEAD_DIM), F32)
    ndiag = tq // tk
    for d in range(ndiag):
        m, l, acc = block(i * ndiag + d, m, l, acc, True)

    def body(j, c):
        return block(j, c[0], c[1], c[2], False)

    m, l, acc = lax.fori_loop(0, i * ndiag, body, (m, l, acc))
    o_ref[...] = (acc / l * _silu(g_ref[...])).astype(o_ref.dtype)


def fox_attention(p, ccol, crow, batch, seq, *, tq=256, tk=256):
    nq = seq // tq
    hb = HEAD_DIM
    return pl.pallas_call(
        functools.partial(_fox_kernel, tq=tq, tk=tk, scale=HEAD_DIM ** -0.5),
        grid=(batch, N_HEADS, nq),
        in_specs=[
            pl.BlockSpec((tq, hb), lambda b, h, i: (b * nq + i, C_FOX_Q // hb + h)),
            pl.BlockSpec((seq, hb), lambda b, h, i: (b, C_FOX_K // hb + h)),
            pl.BlockSpec((seq, hb), lambda b, h, i: (b, C_FOX_V // hb + h)),
            pl.BlockSpec((tq, hb), lambda b, h, i: (b * nq + i, C_FOX_G // hb + h)),
            pl.BlockSpec((tq, LANES), lambda b, h, i: (b * nq + i, 0)),
            pl.BlockSpec((None, 8, seq), lambda b, h, i: (b, 0, 0)),
        ],
        out_specs=pl.BlockSpec((tq, hb), lambda b, h, i: (b * nq + i, h)),
        out_shape=jax.ShapeDtypeStruct((batch * seq, GROUP), BF16),
        scratch_shapes=[pltpu.VMEM((seq, hb), BF16), pltpu.VMEM((seq, hb), BF16)],
        compiler_params=_params("parallel", "parallel", "arbitrary"),
        name="fox_attention",
    )(p, p, p, p, ccol, crow)


def _mla_kernel(q_ref, kn_ref, v_ref, kr_ref, g_ref, cosq_ref, sinq_ref, cos_ref, sin_ref,
                o_ref, kb_ref, vb_ref, *, tq, tk, scale):
    i = pl.program_id(2)

    @pl.when(i == 0)
    def _():
        kb_ref[:, 0:HEAD_DIM] = kn_ref[...]
        kb_ref[:, HEAD_DIM:2 * HEAD_DIM] = _rope(kr_ref[...], cos_ref[...], sin_ref[...]).astype(BF16)
        vb_ref[...] = v_ref[...]

    qn = q_ref[:, 0:HEAD_DIM] * scale
    qr = _rope(q_ref[:, HEAD_DIM:2 * HEAD_DIM], cosq_ref[...], sinq_ref[...]) * scale
    q = jnp.concatenate([qn, qr], axis=1).astype(BF16)
    qpos = i * tq + lax.broadcasted_iota(jnp.int32, (tq, tk), 0)
    col = lax.broadcasted_iota(jnp.int32, (tq, tk), 1)

    def block(j, m, l, acc, masked):
        off = pl.multiple_of(j * tk, tk)
        z = _dot_nt(q, kb_ref[pl.ds(off, tk), :])
        if masked:
            z = jnp.where((off + col) <= qpos, z, NEG_INF)
        return _softmax_step(z, vb_ref[pl.ds(off, tk), :], m, l, acc)

    m = jnp.full((tq, 1), NEG_INF, F32)
    l = jnp.zeros((tq, 1), F32)
    acc = jnp.zeros((tq, HEAD_DIM), F32)
    ndiag = tq // tk
    for d in range(ndiag):
        m, l, acc = block(i * ndiag + d, m, l, acc, True)

    def body(j, c):
        return block(j, c[0], c[1], c[2], False)

    m, l, acc = lax.fori_loop(0, i * ndiag, body, (m, l, acc))
    o_ref[...] = (acc / l * _silu(g_ref[...])).astype(o_ref.dtype)


def mla_attention(p, qp, kvp, cos_m, sin_m, batch, seq, *, tq=256, tk=256):
    nq = seq // tq
    hb = HEAD_DIM
    return pl.pallas_call(
        functools.partial(_mla_kernel, tq=tq, tk=tk, scale=(MLA_NOPE + MLA_ROPE) ** -0.5),
        grid=(batch, N_HEADS, nq),
        in_specs=[
            pl.BlockSpec((tq, 2 * hb), lambda b, h, i: (b * nq + i, h)),
            pl.BlockSpec((seq, hb), lambda b, h, i: (b, 2 * h)),
            pl.BlockSpec((seq, hb), lambda b, h, i: (b, 2 * h + 1)),
            pl.BlockSpec((seq, hb), lambda b, h, i: (b, C_MLA_KR // hb)),
            pl.BlockSpec((tq, hb), lambda b, h, i: (b * nq + i, C_MLA_G // hb + h)),
            pl.BlockSpec((tq, hb), lambda b, h, i: (i, 0)),
            pl.BlockSpec((tq, hb), lambda b, h, i: (i, 0)),
            pl.BlockSpec((seq, hb), lambda b, h, i: (0, 0)),
            pl.BlockSpec((seq, hb), lambda b, h, i: (0, 0)),
        ],
        out_specs=pl.BlockSpec((tq, hb), lambda b, h, i: (b * nq + i, h)),
        out_shape=jax.ShapeDtypeStruct((batch * seq, GROUP), BF16),
        scratch_shapes=[pltpu.VMEM((seq, 2 * hb), BF16), pltpu.VMEM((seq, hb), BF16)],
        compiler_params=_params("parallel", "parallel", "arbitrary"),
        name="mla_attention",
    )(qp, kvp, kvp, p, p, cos_m, sin_m, cos_m, sin_m)


def _compress_kernel(tk_ref, tv_ref, posk_ref, posv_ref, w1k_ref, w1v_ref, w2k_ref, w2v_ref,
                     cos_ref, sin_ref, kc_ref, vc_ref):
    half = tk_ref.shape[1]

    def mlp(t_ref, pos_ref, w1_ref, w2_ref):
        t = t_ref[...]
        a = _dot((t + pos_ref[0:1, :]).astype(BF16), w1_ref[0:half, :])
        bm = _dot((t + pos_ref[1:2, :]).astype(BF16), w1_ref[half:2 * half, :])
        hidden = a + pltpu.roll(bm, bm.shape[0] - 1, 0)
        return _dot(_silu(hidden).astype(BF16), w2_ref[...])

    kc = mlp(tk_ref, posk_ref, w1k_ref, w2k_ref)
    kc_ref[...] = _rope(kc, cos_ref[...], sin_ref[...]).astype(BF16)
    vc_ref[...] = mlp(tv_ref, posv_ref, w1v_ref, w2v_ref).astype(BF16)


def nsa_compress(t2k, t2v, posk, posv, w1k, w1v, w2k, w2v, cos_c, sin_c, batch):
    ng, width = t2k.shape[1], t2k.shape[2]
    full = lambda a: pl.BlockSpec(a.shape, lambda b: (0,) * a.ndim)
    return pl.pallas_call(
        _compress_kernel,
        grid=(batch,),
        in_specs=[
            pl.BlockSpec((None, ng, width), lambda b: (b, 0, 0)),
            pl.BlockSpec((None, ng, width), lambda b: (b, 0, 0)),
            full(posk), full(posv), full(w1k), full(w1v), full(w2k), full(w2v), full(cos_c), full(sin_c),
        ],
        out_specs=[pl.BlockSpec((None, ng, HEAD_DIM), lambda b: (b, 0, 0)),
                   pl.BlockSpec((None, ng, HEAD_DIM), lambda b: (b, 0, 0))],
        out_shape=[jax.ShapeDtypeStruct((batch, ng, HEAD_DIM), BF16),
                   jax.ShapeDtypeStruct((batch, ng, HEAD_DIM), BF16)],
        compiler_params=_params("parallel"),
        name="nsa_compress",
    )(t2k, t2v, posk, posv, w1k, w1v, w2k, w2v, cos_c, sin_c)


def _nsa_kernel(q_ref, ks_ref, vs_ref, kw_ref, vw_ref, kc_ref, vc_ref, misc_ref, g_ref,
                cosq_ref, sinq_ref, cos_ref, sin_ref, c2s_ref, e_ref, o_ref,
                ksb_ref, vsb_ref, kwb_ref, vwb_ref, *, tq, scale, n_sel, chunk):
    i = pl.program_id(1)
    seq = ks_ref.shape[0]
    tk = tq
    nh = N_HEADS

    @pl.when(i == 0)
    def _():
        def body(c, _):
            rows = pl.ds(pl.multiple_of(c * chunk, chunk), chunk)
            cos = cos_ref[rows, :]
            sin = sin_ref[rows, :]
            ksb_ref[rows, :] = _rope(ks_ref[rows, :], cos, sin).astype(BF16)
            kwb_ref[rows, :] = _rope(kw_ref[rows, :], cos, sin).astype(BF16)
            vsb_ref[rows, :] = vs_ref[rows, :].astype(BF16)
            vwb_ref[rows, :] = vw_ref[rows, :].astype(BF16)
            return 0
        lax.fori_loop(0, seq // chunk, body, 0)

    cosq = cosq_ref[...]
    sinq = sinq_ref[...]
    q = jnp.concatenate(
        [(_rope(q_ref[:, h * HEAD_DIM:(h + 1) * HEAD_DIM], cosq, sinq) * scale).astype(BF16)
         for h in range(nh)], axis=0)

    def heads(x):
        return jnp.concatenate([x] * nh, axis=0)

    row = lax.broadcasted_iota(jnp.int32, (tq, LANES), 0)
    lane = lax.broadcasted_iota(jnp.int32, (tq, LANES), 1)
    qpos = i * tq + row
    row_h = lax.broadcasted_iota(jnp.int32, (nh * tq, LANES), 0) & (tq - 1)
    lane_h = lax.broadcasted_iota(jnp.int32, (nh * tq, LANES), 1)
    qpos_h = i * tq + row_h

    zc = _dot_nt(q, kc_ref[...])
    mask_c = (lane_h * CMP_STRIDE + (CMP_LEN - 1)) <= qpos_h
    mc = jnp.max(jnp.where(mask_c, zc, NEG_INF), axis=1, keepdims=True)
    pc = jnp.where(mask_c, jnp.exp(zc - mc), 0.0)
    lc = jnp.sum(pc, axis=1, keepdims=True)
    pc = pc / jnp.where(lc > 0.0, lc, 1.0)
    o_cmp = _dot(pc.astype(BF16), vc_ref[...])

    pc_sum = pc[0:tq]
    for h in range(1, nh):
        pc_sum = pc_sum + pc[h * tq:(h + 1) * tq]
    hi, lo = _split2(pc_sum)
    imp = _dot(hi, c2s_ref[...]) + _dot(lo, c2s_ref[...])
    cur = qpos >> 6
    valid = lane <= cur
    forced = (lane == 0) | (lane == cur) | (lane == cur - 1)
    score = jnp.where(valid, jnp.where(forced, FORCED_BONUS, imp), NEG_INF)
    rank = jnp.zeros((tq, LANES), F32)
    for k in range(n_sel):
        sk = score[:, k:k + 1]
        later = jnp.where(lane > k, 1.0, 0.0)
        rank = rank + jnp.where(sk > score, 1.0, 0.0) + jnp.where(sk == score, later, 0.0)
    sel = jnp.where(rank < float(SEL_TOPN), 1.0, 0.0).astype(BF16)

    assert SEL_LEN == 64 and tk == LANES
    causal = (i * tk + lane_h) <= qpos_h

    m0 = jnp.full((nh * tq, 1), NEG_INF, F32)
    l0 = jnp.zeros((nh * tq, 1), F32)
    a0 = jnp.zeros((nh * tq, HEAD_DIM), F32)

    def sel_block(j, m, l, acc, diag):
        off = pl.multiple_of(j * tk, tk)
        z = _dot_nt(q, ksb_ref[pl.ds(off, tk), :])
        chosen = heads(_dot(sel, e_ref[j]) > 0.5)
        z = jnp.where(chosen, z, NEG_INF)
        if diag:
            z = jnp.where(causal, z, NEG_INF)
        return _softmax_step(z, vsb_ref[pl.ds(off, tk), :], m, l, acc)

    def sel_body(j, c):
        return sel_block(j, c[0], c[1], c[2], False)

    ms, ls, accs = lax.fori_loop(0, i, sel_body, (m0, l0, a0))
    ms, ls, accs = sel_block(i, ms, ls, accs, True)
    o_slc = accs / ls

    mw, lw, accw = m0, l0, a0
    nwin = WINDOW // tk
    for d in range(nwin + 1):
        jj = i - d
        jc = jnp.maximum(jj, 0)
        off = pl.multiple_of(jc * tk, tk)
        z = _dot_nt(q, kwb_ref[pl.ds(off, tk), :])
        if d == 0:
            z = jnp.where(causal, z, NEG_INF)
        else:
            z = jnp.where(jj >= 0, z, NEG_INF)
            if d == nwin:
                z = jnp.where(lane_h > row_h, z, NEG_INF)
        mw, lw, accw = _softmax_step(z, vwb_ref[pl.ds(off, tk), :], mw, lw, accw)
    o_win = accw / lw

    gates = _sigmoid(misc_ref[...])
    for h in range(nh):
        rows = slice(h * tq, (h + 1) * tq)
        c0 = MISC_BRANCH + 3 * h
        o = (gates[:, c0:c0 + 1] * o_cmp[rows] + gates[:, c0 + 1:c0 + 2] * o_slc[rows]
             + gates[:, c0 + 2:c0 + 3] * o_win[rows])
        lanes = slice(h * HEAD_DIM, (h + 1) * HEAD_DIM)
        o_ref[:, lanes] = (o * _silu(g_ref[:, lanes])).astype(o_ref.dtype)


def nsa_attention(p, kc, vc, cos_t, sin_t, batch, seq, *, tq=128):
    nq = seq // tq
    hb = HEAD_DIM
    n_cmp = (seq - CMP_LEN) // CMP_STRIDE + 1
    n_sel = seq // SEL_LEN
    ncp = kc.shape[1]
    assert ncp == LANES and n_cmp <= ncp and n_sel <= LANES and tq == LANES
    cmp_start = np.arange(n_cmp) * CMP_STRIDE
    sel_start = np.arange(n_sel) * SEL_LEN
    overlap = np.clip(np.minimum(cmp_start[:, None] + CMP_LEN, sel_start[None, :] + SEL_LEN)
                      - np.maximum(cmp_start[:, None], sel_start[None, :]), 0, None)
    c2s = np.zeros((ncp, LANES), np.float32)
    c2s[:n_cmp, :n_sel] = overlap / CMP_LEN
    expand = np.zeros((seq // tq, LANES, tq), np.float32)
    keys = np.arange(seq)
    expand[keys // tq, keys // SEL_LEN, keys % tq] = 1.0
    kv = lambda c: pl.BlockSpec((seq, hb), lambda b, i: (b, c // hb))
    return pl.pallas_call(
        functools.partial(_nsa_kernel, tq=tq, scale=HEAD_DIM ** -0.5, n_sel=n_sel, chunk=256),
        grid=(batch, nq),
        in_specs=[
            pl.BlockSpec((tq, GROUP), lambda b, i: (b * nq + i, C_NSA_Q // GROUP)),
            kv(C_NSA_KS), kv(C_NSA_VS), kv(C_NSA_KW), kv(C_NSA_VW),
            pl.BlockSpec((None, ncp, hb), lambda b, i: (b, 0, 0)),
            pl.BlockSpec((None, ncp, hb), lambda b, i: (b, 0, 0)),
            pl.BlockSpec((tq, LANES), lambda b, i: (b * nq + i, C_MISC // LANES)),
            pl.BlockSpec((tq, GROUP), lambda b, i: (b * nq + i, C_NSA_G // GROUP)),
            pl.BlockSpec((tq, hb), lambda b, i: (i, 0)),
            pl.BlockSpec((tq, hb), lambda b, i: (i, 0)),
            pl.BlockSpec((seq, hb), lambda b, i: (0, 0)),
            pl.BlockSpec((seq, hb), lambda b, i: (0, 0)),
            pl.BlockSpec((ncp, LANES), lambda b, i: (0, 0)),
            pl.BlockSpec((seq // tq, LANES, tq), lambda b, i: (0, 0, 0)),
        ],
        out_specs=pl.BlockSpec((tq, GROUP), lambda b, i: (b * nq + i, 0)),
        out_shape=jax.ShapeDtypeStruct((batch * seq, GROUP), BF16),
        scratch_shapes=[pltpu.VMEM((seq, hb), BF16) for _ in range(4)],
        compiler_params=_params("parallel", "arbitrary"),
        name="nsa_attention",
    )(p, p, p, p, p, kc, vc, p, p, cos_t, sin_t, cos_t, sin_t,
      jnp.asarray(c2s, BF16), jnp.asarray(expand, BF16))


def _out_kernel(ma_ref, mb_ref, mc_ref, md_ref, w_ref, g_ref, x_ref, o_ref):
    y = _dot(ma_ref[...], w_ref[0])
    y = y + _dot(mb_ref[...], w_ref[1])
    y = y + _dot(mc_ref[...], w_ref[2])
    y = y + _dot(md_ref[...], w_ref[3])
    ms = jnp.mean(y * y, axis=-1, keepdims=True)
    o_ref[...] = x_ref[...] + y * lax.rsqrt(ms + RMS_EPS) * g_ref[...]


def out_projection(mixes, w, g, x, *, tm=256):
    m, d = x.shape
    mix_spec = pl.BlockSpec((tm, GROUP), lambda i: (i, 0))
    return pl.pallas_call(
        _out_kernel,
        grid=(m // tm,),
        in_specs=[mix_spec, mix_spec, mix_spec, mix_spec,
                  pl.BlockSpec((len(mixes), GROUP, d), lambda i: (0, 0, 0)),
                  pl.BlockSpec((1, d), lambda i: (0, 0)),
                  pl.BlockSpec((tm, d), lambda i: (i, 0))],
        out_specs=pl.BlockSpec((tm, d), lambda i: (i, 0)),
        out_shape=jax.ShapeDtypeStruct((m, d), F32),
        compiler_params=_params("parallel"),
        name="out_projection",
    )(*mixes, w, g, x)


def _spread_rope(w):
    z = jnp.zeros(w.shape[:-1] + (MLA_ROPE // 2,), w.dtype)
    return jnp.concatenate([w[..., :MLA_ROPE // 2], z, w[..., MLA_ROPE // 2:], z], axis=-1)


def _pack_in_projection(w):
    def cols(name):
        o, n = _ORIG[name]
        return w[..., o:o + n]

    def zeros(n):
        return jnp.zeros(w.shape[:-1] + (n,), w.dtype)

    misc = jnp.concatenate([cols("fox_f"), zeros(MISC_BRANCH - 4), cols("nsa_branch"),
                            zeros(LANES - MISC_BRANCH - 12)], axis=-1)
    packed = jnp.concatenate([
        cols("sb_q"), cols("sb_k"), cols("sb_v"), cols("sb_gate"),
        cols("nsa_q"), cols("nsa_k_cmp"), cols("nsa_v_cmp"), cols("nsa_k_sel"), cols("nsa_v_sel"),
        cols("nsa_k_win"), cols("nsa_v_win"), misc, cols("mla_ckv"), cols("nsa_gate"),
        cols("fox_q"), cols("fox_k"), cols("fox_v"), cols("fox_gate"),
        cols("mla_cq"), _spread_rope(cols("mla_k_rope")), cols("mla_gate"),
    ], axis=-1)
    assert packed.shape[-1] == P_WIDTH
    return packed


def _pack_uq(w_uq):
    r = w_uq.shape[0]
    w = w_uq.reshape(r, N_HEADS, MLA_NOPE + MLA_ROPE)
    w = jnp.concatenate([w[..., :MLA_NOPE], _spread_rope(w[..., MLA_NOPE:])], axis=-1)
    return w.reshape(r, N_HEADS * 2 * HEAD_DIM)


def _rope_tables(pos, half, spread):
    inv_freq = ROPE_THETA ** (-jnp.arange(half, dtype=F32) / half)
    ang = pos.astype(F32)[:, None] * inv_freq[None, :]
    cos, sin = jnp.cos(ang), jnp.sin(ang)
    if spread:
        z = jnp.zeros_like(cos)
        return jnp.concatenate([cos, z, cos, z], axis=1), jnp.concatenate([-sin, z, sin, z], axis=1)
    return jnp.concatenate([cos, cos], axis=1), jnp.concatenate([-sin, sin], axis=1)


def _layer(x2, batch, seq, pre_g, post_g, w_in, b_in, w_out, forget_bias,
           pos_k, w1_k, w2_k, pos_v, w1_v, w2_v, q_norm_g, w_uq, kv_norm_g, w_ukv, tables):
    d_model = x2.shape[1]
    cos_t, sin_t, cos_c, sin_c, cos_m, sin_m = tables

    w_p = _pack_in_projection(w_in).astype(BF16)
    b_p = _pack_in_projection(b_in[None, :])
    p = norm_matmul(x2, 0, d_model, pre_g[None, :], w_p, b_p, tm=1024, tn=512, out_dtype=F32)

    o_sb = sb_attention(p, batch, seq)

    ng = seq // CMP_STRIDE
    width = CMP_STRIDE * HEAD_DIM
    t2k = p[:, C_NSA_KC:C_NSA_KC + HEAD_DIM].reshape(batch, ng, width)
    t2v = p[:, C_NSA_VC:C_NSA_VC + HEAD_DIM].reshape(batch, ng, width)
    kc, vc = nsa_compress(t2k, t2v, pos_k.reshape(2, width), pos_v.reshape(2, width),
                          w1_k.astype(BF16), w1_v.astype(BF16), w2_k.astype(BF16), w2_v.astype(BF16),
                          cos_c, sin_c, batch)
    o_nsa = nsa_attention(p, kc, vc, cos_t, sin_t, batch, seq)

    bias_row = jnp.zeros((1, LANES), F32).at[0, MISC_FOX:MISC_FOX + N_HEADS].set(forget_bias)
    ccol, crow = fox_prep(p, bias_row, batch, seq)
    o_fox = fox_attention(p, ccol, crow, batch, seq)

    qp = norm_matmul(p, C_MLA_CQ // MLA_Q_RANK, MLA_Q_RANK, q_norm_g[None, :], _pack_uq(w_uq).astype(BF16),
                     jnp.zeros((1, N_HEADS * 2 * HEAD_DIM), F32), tm=1024, tn=1024, out_dtype=F32)
    kvp = norm_matmul(p, C_MLA_CKV // MLA_KV_RANK, MLA_KV_RANK, kv_norm_g[None, :], w_ukv.astype(BF16),
                      jnp.zeros((1, N_HEADS * 2 * HEAD_DIM), F32), tm=1024, tn=1024, out_dtype=BF16)
    o_mla = mla_attention(p, qp, kvp, cos_m, sin_m, batch, seq)

    w_o = w_out.astype(BF16).reshape(4, GROUP, d_model)
    return out_projection((o_sb, o_nsa, o_fox, o_mla), w_o, post_g[None, :], x2)


def kernel(x, pre_norm_g, post_norm_g, w_in, b_in, w_out, fox_forget_bias, nsa_cmp_pos_k, nsa_cmp_w1_k, nsa_cmp_w2_k, nsa_cmp_pos_v, nsa_cmp_w1_v, nsa_cmp_w2_v, mla_q_norm_g, mla_w_uq, mla_kv_norm_g, mla_w_ukv):
    batch, seq, d_model = x.shape
    depth = w_in.shape[0]
    pos = jnp.arange(seq)
    cmp_end = jnp.arange(seq // CMP_STRIDE) * CMP_STRIDE + (CMP_LEN - 1)
    tables = (_rope_tables(pos, HEAD_DIM // 2, False) + _rope_tables(cmp_end, HEAD_DIM // 2, False)
              + _rope_tables(pos, MLA_ROPE // 2, True))
    x2 = x.reshape(batch * seq, d_model)
    for l in range(depth):
        x2 = _layer(x2, batch, seq, pre_norm_g[l], post_norm_g[l], w_in[l], b_in[l], w_out[l],
                    fox_forget_bias[l], nsa_cmp_pos_k[l], nsa_cmp_w1_k[l], nsa_cmp_w2_k[l],
                    nsa_cmp_pos_v[l], nsa_cmp_w1_v[l], nsa_cmp_w2_v[l],
                    mla_q_norm_g[l], mla_w_uq[l], mla_kv_norm_g[l], mla_w_ukv[l], tables)
    return x2.reshape(batch, seq, d_model)
```

```python
import functools

import numpy as np
import jax
import jax.numpy as jnp
from jax import lax
from jax.experimental import pallas as pl
from jax.experimental.pallas import tpu as pltpu

F32 = jnp.float32
BF16 = jnp.bfloat16

LANES = 128
HEAD_DIM = 128
N_HEADS = 4
GROUP = N_HEADS * HEAD_DIM
RMS_EPS = 1e-6
NEG_INF = -1e30
ROPE_THETA = 10000.0

CMP_LEN = 32
CMP_STRIDE = 16
SEL_LEN = 64
SEL_TOPN = 16
WINDOW = 512
FORCED_BONUS = 1e6

MLA_Q_RANK = 384
MLA_KV_RANK = 128
MLA_NOPE = 128
MLA_ROPE = 64

VMEM_LIMIT = 56 * 1024 * 1024

C_SB_Q, C_SB_K, C_SB_V, C_SB_G = 0, 512, 1024, 1536
C_NSA_Q = 2048
C_NSA_KC, C_NSA_VC, C_NSA_KS, C_NSA_VS, C_NSA_KW, C_NSA_VW = 2560, 2688, 2816, 2944, 3072, 3200
C_MISC = 3328
C_MLA_CKV = 3456
C_NSA_G = 3584
C_FOX_Q, C_FOX_K, C_FOX_V, C_FOX_G = 4096, 4608, 5120, 5632
C_MLA_CQ = 6144
C_MLA_KR = 6528
C_MLA_G = 6656
P_WIDTH = 7168
assert all(c % GROUP == 0 for c in (C_SB_Q, C_SB_K, C_SB_V, C_SB_G, C_NSA_Q, C_NSA_G, C_FOX_Q, C_FOX_K,
                                    C_FOX_V, C_FOX_G, C_MLA_G)) and C_MLA_CQ % MLA_Q_RANK == 0
MISC_FOX = 0
MISC_BRANCH = 8

_ORIG = {}
_off = 0
for _name, _w in (("sb_q", 512), ("sb_k", 512), ("sb_v", 512), ("sb_gate", 512),
                  ("nsa_q", 512), ("nsa_k_cmp", 128), ("nsa_v_cmp", 128), ("nsa_k_sel", 128),
                  ("nsa_v_sel", 128), ("nsa_k_win", 128), ("nsa_v_win", 128), ("nsa_branch", 12),
                  ("nsa_gate", 512), ("fox_q", 512), ("fox_k", 512), ("fox_v", 512), ("fox_f", 4),
                  ("fox_gate", 512), ("mla_cq", 384), ("mla_ckv", 128), ("mla_k_rope", 64),
                  ("mla_gate", 512)):
    _ORIG[_name] = (_off, _w)
    _off += _w
IN_WIDTH = _off


def _params(*sem):
    return pltpu.CompilerParams(dimension_semantics=sem, vmem_limit_bytes=VMEM_LIMIT)


def _dot(a, b):
    return jnp.dot(a, b, preferred_element_type=F32)


def _dot_nt(a, b):
    return lax.dot_general(a, b, (((1,), (1,)), ((), ())), preferred_element_type=F32)


def _split2(x):
    hi = x.astype(BF16)
    lo = (x - hi.astype(F32)).astype(BF16)
    return hi, lo


def _split3(x):
    hi = x.astype(BF16)
    r = x - hi.astype(F32)
    mid = r.astype(BF16)
    lo = (r - mid.astype(F32)).astype(BF16)
    return hi, mid, lo


def _rope(x, cos, sin_signed):
    return x * cos + pltpu.roll(x, 64, 1) * sin_signed


def _silu(g):
    return g / (1.0 + jnp.exp(-g))


def _sigmoid(g):
    return 1.0 / (1.0 + jnp.exp(-g))


def _softmax_step(z, v, m, l, acc):
    m_new = jnp.maximum(m, jnp.max(z, axis=1, keepdims=True))
    alpha = jnp.exp(m - m_new)
    p = jnp.exp(z - m_new)
    l = alpha * l + jnp.sum(p, axis=1, keepdims=True)
    acc = alpha * acc + _dot(p.astype(BF16), v)
    return m_new, l, acc


def _init_state(rows, nh):
    return tuple((jnp.full((rows, 1), NEG_INF, F32), jnp.zeros((rows, 1), F32), jnp.zeros((rows, HEAD_DIM), F32))
                 for _ in range(nh))


def _cast_rows(dst_ref, src_ref, chunk):
    def body(c, _):
        rows = pl.ds(pl.multiple_of(c * chunk, chunk), chunk)
        dst_ref[rows, :] = src_ref[rows, :].astype(dst_ref.dtype)
        return 0
    lax.fori_loop(0, src_ref.shape[0] // chunk, body, 0)


def _norm_matmul_kernel(x_ref, g_ref, w_ref, b_ref, o_ref, h_ref, *, chunk):
    tm = x_ref.shape[0]

    @pl.when(pl.program_id(1) == 0)
    def _():
        def body(c, _):
            rows = pl.ds(pl.multiple_of(c * chunk, chunk), chunk)
            x = x_ref[rows, :]
            ms = jnp.mean(x * x, axis=-1, keepdims=True)
            h_ref[rows, :] = (x * lax.rsqrt(ms + RMS_EPS) * g_ref[...]).astype(BF16)
            return 0
        lax.fori_loop(0, tm // chunk, body, 0)

    o_ref[...] = (_dot(h_ref[...], w_ref[...]) + b_ref[...]).astype(o_ref.dtype)


def norm_matmul(x, xcol, kdim, g, w, b, *, tm, tn, out_dtype):
    m = x.shape[0]
    n = w.shape[1]
    chunk = min(tm, 256)
    return pl.pallas_call(
        functools.partial(_norm_matmul_kernel, chunk=chunk),
        grid=(m // tm, n // tn),
        in_specs=[
            pl.BlockSpec((tm, kdim), lambda i, j: (i, xcol)),
            pl.BlockSpec((1, kdim), lambda i, j: (0, 0)),
            pl.BlockSpec((kdim, tn), lambda i, j: (0, j)),
            pl.BlockSpec((1, tn), lambda i, j: (0, j)),
        ],
        out_specs=pl.BlockSpec((tm, tn), lambda i, j: (i, j)),
        out_shape=jax.ShapeDtypeStruct((m, n), out_dtype),
        scratch_shapes=[pltpu.VMEM((tm, kdim), BF16)],
        compiler_params=_params("parallel", "arbitrary"),
        name="norm_matmul",
    )(x, g, w, b)


def _sb_kernel(q_ref, k_ref, v_ref, g_ref, u_ref, o_ref, kb_ref, vb_ref, *, tq, tk, scale):
    i = pl.program_id(1)
    nh = N_HEADS
    nsub = tq // tk

    @pl.when(i == 0)
    def _():
        _cast_rows(kb_ref, k_ref, 256)
        _cast_rows(vb_ref, v_ref, 256)

    u = u_ref[...]
    qs = [(q_ref[:, h * HEAD_DIM:(h + 1) * HEAD_DIM] * scale).astype(BF16) for h in range(nh)]
    qpos = i * tq + lax.broadcasted_iota(jnp.int32, (tq, tk), 0)
    col = lax.broadcasted_iota(jnp.int32, (tq, tk), 1)

    def group(goff, carries, accs, masked):
        new_carries, new_accs = [], []
        for h in range(nh):
            lanes = slice(h * HEAD_DIM, (h + 1) * HEAD_DIM)
            carry = carries[h]
            ws = [None] * nsub
            for c in reversed(range(nsub)):
                off = goff + c * tk
                z = _dot_nt(qs[h], kb_ref[pl.ds(off, tk), lanes])
                softplus = jnp.maximum(z, 0.0) + jnp.log(1.0 + jnp.exp(-jnp.abs(z)))
                log_keep = -softplus
                log_beta = z - softplus
                if masked:
                    mask = (off + col) < qpos
                    log_keep = jnp.where(mask, log_keep, 0.0)
                hi, lo = _split2(log_keep)
                r = _dot(jnp.concatenate([hi, lo], axis=0), u)
                log_after = r[:tq] + r[tq:]
                w = jnp.exp(log_beta + log_after + carry)
                if masked:
                    w = jnp.where(mask, w, 0.0)
                ws[c] = w.astype(BF16)
                carry = carry + jnp.sum(log_keep, axis=1, keepdims=True)
            w = jnp.concatenate(ws, axis=1)
            new_accs.append(accs[h] + _dot(w, vb_ref[pl.ds(goff, tq), lanes]))
            new_carries.append(carry)
        return tuple(new_carries), tuple(new_accs)

    carries = tuple(jnp.zeros((tq, 1), F32) for _ in range(nh))
    accs = tuple(jnp.zeros((tq, HEAD_DIM), F32) for _ in range(nh))
    carries, accs = group(pl.multiple_of(i * tq, tq), carries, accs, True)

    def body(t, c):
        return group(pl.multiple_of((i - 1 - t) * tq, tq), c[0], c[1], False)

    carries, accs = lax.fori_loop(0, i, body, (carries, accs))
    for h in range(nh):
        lanes = slice(h * HEAD_DIM, (h + 1) * HEAD_DIM)
        o_ref[:, lanes] = (accs[h] * _silu(g_ref[:, lanes])).astype(o_ref.dtype)


def sb_attention(p, batch, seq, *, tq=256, tk=128):
    nq = seq // tq
    u = jnp.asarray(np.tril(np.ones((tk, tk), np.float32), -1), BF16)
    return pl.pallas_call(
        functools.partial(_sb_kernel, tq=tq, tk=tk, scale=HEAD_DIM ** -0.5),
        grid=(batch, nq),
        in_specs=[
            pl.BlockSpec((tq, GROUP), lambda b, i: (b * nq + i, C_SB_Q // GROUP)),
            pl.BlockSpec((seq, GROUP), lambda b, i: (b, C_SB_K // GROUP)),
            pl.BlockSpec((seq, GROUP), lambda b, i: (b, C_SB_V // GROUP)),
            pl.BlockSpec((tq, GROUP), lambda b, i: (b * nq + i, C_SB_G // GROUP)),
            pl.BlockSpec((tk, tk), lambda b, i: (0, 0)),
        ],
        out_specs=pl.BlockSpec((tq, GROUP), lambda b, i: (b * nq + i, 0)),
        out_shape=jax.ShapeDtypeStruct((batch * seq, GROUP), BF16),
        scratch_shapes=[pltpu.VMEM((seq, GROUP), BF16), pltpu.VMEM((seq, GROUP), BF16)],
        compiler_params=_params("parallel", "arbitrary"),
        name="sb_attention",
    )(p, p, p, p, u)


def _fox_prep_kernel(f_ref, bias_ref, col_ref, row_ref, *, chunk):
    seq = f_ref.shape[0]
    r = lax.broadcasted_iota(jnp.int32, (chunk, chunk), 0)
    c = lax.broadcasted_iota(jnp.int32, (chunk, chunk), 1)
    tri = jnp.where(c <= r, 1.0, 0.0).astype(BF16)
    carry = jnp.zeros((1, LANES), F32)
    for n in range(seq // chunk):
        x = f_ref[n * chunk:(n + 1) * chunk, :] + bias_ref[...]
        logf = jnp.minimum(x, 0.0) - jnp.log(1.0 + jnp.exp(-jnp.abs(x)))
        hi, mid, lo = _split3(logf)
        within = _dot(tri, hi) + _dot(tri, mid) + _dot(tri, lo)
        cum = within + carry
        col_ref[n * chunk:(n + 1) * chunk, :] = cum
        row_ref[:, n * chunk:(n + 1) * chunk] = cum.T[0:8, :]
        carry = cum[chunk - 1:chunk, :]


def fox_prep(p, bias_row, batch, seq):
    chunk = LANES
    return pl.pallas_call(
        functools.partial(_fox_prep_kernel, chunk=chunk),
        grid=(batch,),
        in_specs=[
            pl.BlockSpec((seq, LANES), lambda b: (b, C_MISC // LANES)),
            pl.BlockSpec((1, LANES), lambda b: (0, 0)),
        ],
        out_specs=[
            pl.BlockSpec((seq, LANES), lambda b: (b, 0)),
            pl.BlockSpec((None, 8, seq), lambda b: (b, 0, 0)),
        ],
        out_shape=[jax.ShapeDtypeStruct((batch * seq, LANES), F32),
                   jax.ShapeDtypeStruct((batch, 8, seq), F32)],
        compiler_params=_params("parallel"),
        name="fox_prep",
    )(p, bias_row)


def _fox_kernel(q_ref, k_ref, v_ref, g_ref, ccol_ref, crow_ref, o_ref, kb_ref, vb_ref, *, tq, tk, scale):
    i = pl.program_id(1)
    nh = N_HEADS

    @pl.when(i == 0)
    def _():
        _cast_rows(kb_ref, k_ref, 256)
        _cast_rows(vb_ref, v_ref, 256)

    qs = [(q_ref[:, h * HEAD_DIM:(h + 1) * HEAD_DIM] * scale).astype(BF16) for h in range(nh)]
    cqs = [ccol_ref[:, h:h + 1] for h in range(nh)]
    qpos = i * tq + lax.broadcasted_iota(jnp.int32, (tq, tk), 0)
    col = lax.broadcasted_iota(jnp.int32, (tq, tk), 1)

    def block(j, state, masked):
        off = pl.multiple_of(j * tk, tk)
        out = []
        for h in range(nh):
            lanes = slice(h * HEAD_DIM, (h + 1) * HEAD_DIM)
            ck = crow_ref[h:h + 1, pl.ds(off, tk)]
            z = _dot_nt(qs[h], kb_ref[pl.ds(off, tk), lanes]) + cqs[h] - ck
            if masked:
                z = jnp.where((off + col) <= qpos, z, NEG_INF)
            out.append(_softmax_step(z, vb_ref[pl.ds(off, tk), lanes], *state[h]))
        return tuple(out)

    state = block(i, _init_state(tq, nh), True)
    state = lax.fori_loop(0, i, lambda j, st: block(j, st, False), state)
    for h in range(nh):
        lanes = slice(h * HEAD_DIM, (h + 1) * HEAD_DIM)
        _, l, acc = state[h]
        o_ref[:, lanes] = (acc / l * _silu(g_ref[:, lanes])).astype(o_ref.dtype)


def fox_attention(p, ccol, crow, batch, seq, *, tq=256):
    nq = seq // tq
    return pl.pallas_call(
        functools.partial(_fox_kernel, tq=tq, tk=tq, scale=HEAD_DIM ** -0.5),
        grid=(batch, nq),
        in_specs=[
            pl.BlockSpec((tq, GROUP), lambda b, i: (b * nq + i, C_FOX_Q // GROUP)),
            pl.BlockSpec((seq, GROUP), lambda b, i: (b, C_FOX_K // GROUP)),
            pl.BlockSpec((seq, GROUP), lambda b, i: (b, C_FOX_V // GROUP)),
            pl.BlockSpec((tq, GROUP), lambda b, i: (b * nq + i, C_FOX_G // GROUP)),
            pl.BlockSpec((tq, LANES), lambda b, i: (b * nq + i, 0)),
            pl.BlockSpec((None, 8, seq), lambda b, i: (b, 0, 0)),
        ],
        out_specs=pl.BlockSpec((tq, GROUP), lambda b, i: (b * nq + i, 0)),
        out_shape=jax.ShapeDtypeStruct((batch * seq, GROUP), BF16),
        scratch_shapes=[pltpu.VMEM((seq, GROUP), BF16), pltpu.VMEM((seq, GROUP), BF16)],
        compiler_params=_params("parallel", "arbitrary"),
        name="fox_attention",
    )(p, p, p, p, ccol, crow)


def _mla_kernel(q_ref, kv_ref, kr_ref, g_ref, cosq_ref, sinq_ref, cos_ref, sin_ref,
                o_ref, kb_ref, *, tq, tk, scale):
    i = pl.program_id(1)
    nh = N_HEADS
    hw = 2 * HEAD_DIM
    seq = kv_ref.shape[0]

    @pl.when(i == 0)
    def _():
        def body(c, _):
            rows = pl.ds(pl.multiple_of(c * 256, 256), 256)
            kr = _rope(kr_ref[rows, :], cos_ref[rows, :], sin_ref[rows, :]).astype(BF16)
            for h in range(nh):
                kb_ref[rows, h * hw:h * hw + HEAD_DIM] = kv_ref[rows, h * hw:h * hw + HEAD_DIM]
                kb_ref[rows, h * hw + HEAD_DIM:(h + 1) * hw] = kr
            return 0
        lax.fori_loop(0, seq // 256, body, 0)

    cosq = cosq_ref[...]
    sinq = sinq_ref[...]
    qs = []
    for h in range(nh):
        qn = q_ref[:, h * hw:h * hw + HEAD_DIM] * scale
        qr = _rope(q_ref[:, h * hw + HEAD_DIM:(h + 1) * hw], cosq, sinq) * scale
        qs.append(jnp.concatenate([qn, qr], axis=1).astype(BF16))
    qpos = i * tq + lax.broadcasted_iota(jnp.int32, (tq, tk), 0)
    col = lax.broadcasted_iota(jnp.int32, (tq, tk), 1)

    def block(j, state, masked):
        off = pl.multiple_of(j * tk, tk)
        out = []
        for h in range(nh):
            z = _dot_nt(qs[h], kb_ref[pl.ds(off, tk), h * hw:(h + 1) * hw])
            if masked:
                z = jnp.where((off + col) <= qpos, z, NEG_INF)
            out.append(_softmax_step(z, kv_ref[pl.ds(off, tk), h * hw + HEAD_DIM:(h + 1) * hw], *state[h]))
        return tuple(out)

    state = block(i, _init_state(tq, nh), True)
    state = lax.fori_loop(0, i, lambda j, st: block(j, st, False), state)
    for h in range(nh):
        lanes = slice(h * HEAD_DIM, (h + 1) * HEAD_DIM)
        _, l, acc = state[h]
        o_ref[:, lanes] = (acc / l * _silu(g_ref[:, lanes])).astype(o_ref.dtype)


def mla_attention(p, qp, kvp, cos_m, sin_m, batch, seq, *, tq=256):
    nq = seq // tq
    hb = HEAD_DIM
    wide = N_HEADS * 2 * hb
    return pl.pallas_call(
        functools.partial(_mla_kernel, tq=tq, tk=tq, scale=(MLA_NOPE + MLA_ROPE) ** -0.5),
        grid=(batch, nq),
        in_specs=[
            pl.BlockSpec((tq, wide), lambda b, i: (b * nq + i, 0)),
            pl.BlockSpec((seq, wide), lambda b, i: (b, 0)),
            pl.BlockSpec((seq, hb), lambda b, i: (b, C_MLA_KR // hb)),
            pl.BlockSpec((tq, GROUP), lambda b, i: (b * nq + i, C_MLA_G // GROUP)),
            pl.BlockSpec((tq, hb), lambda b, i: (i, 0)),
            pl.BlockSpec((tq, hb), lambda b, i: (i, 0)),
            pl.BlockSpec((seq, hb), lambda b, i: (0, 0)),
            pl.BlockSpec((seq, hb), lambda b, i: (0, 0)),
        ],
        out_specs=pl.BlockSpec((tq, GROUP), lambda b, i: (b * nq + i, 0)),
        out_shape=jax.ShapeDtypeStruct((batch * seq, GROUP), BF16),
        scratch_shapes=[pltpu.VMEM((seq, wide), BF16)],
        compiler_params=_params("parallel", "arbitrary"),
        name="mla_attention",
    )(qp, kvp, p, p, cos_m, sin_m, cos_m, sin_m)


def _compress_kernel(tk_ref, tv_ref, posk_ref, posv_ref, w1k_ref, w1v_ref, w2k_ref, w2v_ref,
                     cos_ref, sin_ref, kc_ref, vc_ref):
    half = tk_ref.shape[1]

    def mlp(t_ref, pos_ref, w1_ref, w2_ref):
        t = t_ref[...]
        a = _dot((t + pos_ref[0:1, :]).astype(BF16), w1_ref[0:half, :])
        bm = _dot((t + pos_ref[1:2, :]).astype(BF16), w1_ref[half:2 * half, :])
        hidden = a + pltpu.roll(bm, bm.shape[0] - 1, 0)
        return _dot(_silu(hidden).astype(BF16), w2_ref[...])

    kc = mlp(tk_ref, posk_ref, w1k_ref, w2k_ref)
    kc_ref[...] = _rope(kc, cos_ref[...], sin_ref[...]).astype(BF16)
    vc_ref[...] = mlp(tv_ref, posv_ref, w1v_ref, w2v_ref).astype(BF16)


def nsa_compress(t2k, t2v, posk, posv, w1k, w1v, w2k, w2v, cos_c, sin_c, batch):
    ng, width = t2k.shape[1], t2k.shape[2]
    full = lambda a: pl.BlockSpec(a.shape, lambda b: (0,) * a.ndim)
    return pl.pallas_call(
        _compress_kernel,
        grid=(batch,),
        in_specs=[
            pl.BlockSpec((None, ng, width), lambda b: (b, 0, 0)),
            pl.BlockSpec((None, ng, width), lambda b: (b, 0, 0)),
            full(posk), full(posv), full(w1k), full(w1v), full(w2k), full(w2v), full(cos_c), full(sin_c),
        ],
        out_specs=[pl.BlockSpec((None, ng, HEAD_DIM), lambda b: (b, 0, 0)),
                   pl.BlockSpec((None, ng, HEAD_DIM), lambda b: (b, 0, 0))],
        out_shape=[jax.ShapeDtypeStruct((batch, ng, HEAD_DIM), BF16),
                   jax.ShapeDtypeStruct((batch, ng, HEAD_DIM), BF16)],
        compiler_params=_params("parallel"),
        name="nsa_compress",
    )(t2k, t2v, posk, posv, w1k, w1v, w2k, w2v, cos_c, sin_c)


def _nsa_kernel(q_ref, ks_ref, vs_ref, kw_ref, vw_ref, kc_ref, vc_ref, misc_ref, g_ref,
                cosq_ref, sinq_ref, cos_ref, sin_ref, c2s_ref, e_ref, o_ref,
                ksb_ref, vsb_ref, kwb_ref, vwb_ref, *, tq, scale, n_sel, chunk):
    i = pl.program_id(1)
    seq = ks_ref.shape[0]
    tk = tq
    nh = N_HEADS
    rows_all = nh * tq

    @pl.when(i == 0)
    def _():
        def body(c, _):
            rows = pl.ds(pl.multiple_of(c * chunk, chunk), chunk)
            cos = cos_ref[rows, :]
            sin = sin_ref[rows, :]
            ksb_ref[rows, :] = _rope(ks_ref[rows, :], cos, sin).astype(BF16)
            kwb_ref[rows, :] = _rope(kw_ref[rows, :], cos, sin).astype(BF16)
            vsb_ref[rows, :] = vs_ref[rows, :].astype(BF16)
            vwb_ref[rows, :] = vw_ref[rows, :].astype(BF16)
            return 0
        lax.fori_loop(0, seq // chunk, body, 0)

    cosq = cosq_ref[...]
    sinq = sinq_ref[...]
    q = jnp.concatenate(
        [(_rope(q_ref[:, h * HEAD_DIM:(h + 1) * HEAD_DIM], cosq, sinq) * scale).astype(BF16)
         for h in range(nh)], axis=0)

    def heads(x):
        return jnp.concatenate([x] * nh, axis=0)

    row_c = lax.broadcasted_iota(jnp.int32, (rows_all, LANES), 0) & (tq - 1)
    lane_c = lax.broadcasted_iota(jnp.int32, (rows_all, LANES), 1)
    zc = _dot_nt(q, kc_ref[...])
    mask_c = (lane_c * CMP_STRIDE + (CMP_LEN - 1)) <= (i * tq + row_c)
    mc = jnp.max(jnp.where(mask_c, zc, NEG_INF), axis=1, keepdims=True)
    pc = jnp.where(mask_c, jnp.exp(zc - mc), 0.0)
    lc = jnp.sum(pc, axis=1, keepdims=True)
    pc = pc / jnp.where(lc > 0.0, lc, 1.0)
    o_cmp = _dot(pc.astype(BF16), vc_ref[...])

    pc_sum = pc[0:tq]
    for h in range(1, nh):
        pc_sum = pc_sum + pc[h * tq:(h + 1) * tq]
    hi, lo = _split2(pc_sum)
    imp = _dot_nt(c2s_ref[...], hi) + _dot_nt(c2s_ref[...], lo)
    blk = lax.broadcasted_iota(jnp.int32, (n_sel, tq), 0)
    cur = (i * tq + lax.broadcasted_iota(jnp.int32, (n_sel, tq), 1)) >> 6
    valid = blk <= cur
    forced = (blk == 0) | (blk == cur) | (blk == cur - 1)
    score = jnp.where(valid, jnp.where(forced, FORCED_BONUS, imp), NEG_INF)
    rank = jnp.zeros((n_sel, tq), F32)
    for k in range(n_sel):
        sk = score[k:k + 1, :]
        later = jnp.where(blk > k, 1.0, 0.0)
        rank = rank + jnp.where(sk > score, 1.0, 0.0) + jnp.where(sk == score, later, 0.0)
    sel_t = jnp.where(rank < float(SEL_TOPN), 1.0, 0.0)
    sel_t = jnp.concatenate([sel_t, jnp.zeros((LANES - n_sel, tq), F32)], axis=0)
    sel = jnp.concatenate([sel_t[:, c * LANES:(c + 1) * LANES].T for c in range(tq // LANES)],
                          axis=0).astype(BF16)

    row_k = lax.broadcasted_iota(jnp.int32, (rows_all, tk), 0) & (tq - 1)
    col_k = lax.broadcasted_iota(jnp.int32, (rows_all, tk), 1)
    causal = col_k <= row_k

    state0 = _init_state(rows_all, 1)[0]

    def sel_block(j, st, diag):
        off = pl.multiple_of(j * tk, tk)
        z = _dot_nt(q, ksb_ref[pl.ds(off, tk), :])
        chosen = heads(_dot(sel, e_ref[j])) > 0.5
        z = jnp.where(chosen, z, NEG_INF)
        if diag:
            z = jnp.where(causal, z, NEG_INF)
        return _softmax_step(z, vsb_ref[pl.ds(off, tk), :], *st)

    st = lax.fori_loop(0, i, lambda j, s: sel_block(j, s, False), state0)
    _, ls, accs = sel_block(i, st, True)
    o_slc = accs / ls

    st = state0
    nwin = WINDOW // tk
    for d in range(nwin + 1):
        jj = i - d
        off = pl.multiple_of(jnp.maximum(jj, 0) * tk, tk)
        z = _dot_nt(q, kwb_ref[pl.ds(off, tk), :])
        if d == 0:
            z = jnp.where(causal, z, NEG_INF)
        else:
            z = jnp.where(jj >= 0, z, NEG_INF)
            if d == nwin:
                z = jnp.where(col_k > row_k, z, NEG_INF)
        st = _softmax_step(z, vwb_ref[pl.ds(off, tk), :], *st)
    o_win = st[2] / st[1]

    gates = _sigmoid(misc_ref[...])
    for h in range(nh):
        rows = slice(h * tq, (h + 1) * tq)
        c0 = MISC_BRANCH + 3 * h
        o = (gates[:, c0:c0 + 1] * o_cmp[rows] + gates[:, c0 + 1:c0 + 2] * o_slc[rows]
             + gates[:, c0 + 2:c0 + 3] * o_win[rows])
        lanes = slice(h * HEAD_DIM, (h + 1) * HEAD_DIM)
        o_ref[:, lanes] = (o * _silu(g_ref[:, lanes])).astype(o_ref.dtype)


def nsa_attention(p, kc, vc, cos_t, sin_t, batch, seq, *, tq=256):
    nq = seq // tq
    hb = HEAD_DIM
    n_cmp = (seq - CMP_LEN) // CMP_STRIDE + 1
    n_sel = seq // SEL_LEN
    ncp = kc.shape[1]
    assert ncp == LANES and n_cmp <= ncp and n_sel <= LANES and n_sel % 16 == 0
    assert SEL_LEN == 64 and WINDOW % tq == 0 and tq % LANES == 0 and tq & (tq - 1) == 0
    cmp_start = np.arange(n_cmp) * CMP_STRIDE
    sel_start = np.arange(n_sel) * SEL_LEN
    overlap = np.clip(np.minimum(cmp_start[:, None] + CMP_LEN, sel_start[None, :] + SEL_LEN)
                      - np.maximum(cmp_start[:, None], sel_start[None, :]), 0, None)
    c2s = np.zeros((n_sel, ncp), np.float32)
    c2s[:, :n_cmp] = (overlap / CMP_LEN).T
    expand = np.zeros((seq // tq, LANES, tq), np.float32)
    keys = np.arange(seq)
    expand[keys // tq, keys // SEL_LEN, keys % tq] = 1.0
    kv = lambda c: pl.BlockSpec((seq, hb), lambda b, i: (b, c // hb))
    return pl.pallas_call(
        functools.partial(_nsa_kernel, tq=tq, scale=HEAD_DIM ** -0.5, n_sel=n_sel, chunk=256),
        grid=(batch, nq),
        in_specs=[
            pl.BlockSpec((tq, GROUP), lambda b, i: (b * nq + i, C_NSA_Q // GROUP)),
            kv(C_NSA_KS), kv(C_NSA_VS), kv(C_NSA_KW), kv(C_NSA_VW),
            pl.BlockSpec((None, ncp, hb), lambda b, i: (b, 0, 0)),
            pl.BlockSpec((None, ncp, hb), lambda b, i: (b, 0, 0)),
            pl.BlockSpec((tq, LANES), lambda b, i: (b * nq + i, C_MISC // LANES)),
            pl.BlockSpec((tq, GROUP), lambda b, i: (b * nq + i, C_NSA_G // GROUP)),
            pl.BlockSpec((tq, hb), lambda b, i: (i, 0)),
            pl.BlockSpec((tq, hb), lambda b, i: (i, 0)),
            pl.BlockSpec((seq, hb), lambda b, i: (0, 0)),
            pl.BlockSpec((seq, hb), lambda b, i: (0, 0)),
            pl.BlockSpec((n_sel, ncp), lambda b, i: (0, 0)),
            pl.BlockSpec((seq // tq, LANES, tq), lambda b, i: (0, 0, 0)),
        ],
        out_specs=pl.BlockSpec((tq, GROUP), lambda b, i: (b * nq + i, 0)),
        out_shape=jax.ShapeDtypeStruct((batch * seq, GROUP), BF16),
        scratch_shapes=[pltpu.VMEM((seq, hb), BF16) for _ in range(4)],
        compiler_params=_params("parallel", "arbitrary"),
        name="nsa_attention",
    )(p, p, p, p, p, kc, vc, p, p, cos_t, sin_t, cos_t, sin_t,
      jnp.asarray(c2s, BF16), jnp.asarray(expand, BF16))


def _out_kernel(ma_ref, mb_ref, mc_ref, md_ref, w_ref, g_ref, x_ref, o_ref):
    y = _dot(ma_ref[...], w_ref[0])
    y = y + _dot(mb_ref[...], w_ref[1])
    y = y + _dot(mc_ref[...], w_ref[2])
    y = y + _dot(md_ref[...], w_ref[3])
    ms = jnp.mean(y * y, axis=-1, keepdims=True)
    o_ref[...] = x_ref[...] + y * lax.rsqrt(ms + RMS_EPS) * g_ref[...]


def out_projection(mixes, w, g, x, *, tm=256):
    m, d = x.shape
    mix_spec = pl.BlockSpec((tm, GROUP), lambda i: (i, 0))
    return pl.pallas_call(
        _out_kernel,
        grid=(m // tm,),
        in_specs=[mix_spec, mix_spec, mix_spec, mix_spec,
                  pl.BlockSpec((len(mixes), GROUP, d), lambda i: (0, 0, 0)),
                  pl.BlockSpec((1, d), lambda i: (0, 0)),
                  pl.BlockSpec((tm, d), lambda i: (i, 0))],
        out_specs=pl.BlockSpec((tm, d), lambda i: (i, 0)),
        out_shape=jax.ShapeDtypeStruct((m, d), F32),
        compiler_params=_params("parallel"),
        name="out_projection",
    )(*mixes, w, g, x)


def _spread_rope(w):
    z = jnp.zeros(w.shape[:-1] + (MLA_ROPE // 2,), w.dtype)
    return jnp.concatenate([w[..., :MLA_ROPE // 2], z, w[..., MLA_ROPE // 2:], z], axis=-1)


def _pack_in_projection(w):
    def cols(name):
        o, n = _ORIG[name]
        return w[..., o:o + n]

    def zeros(n):
        return jnp.zeros(w.shape[:-1] + (n,), w.dtype)

    misc = jnp.concatenate([cols("fox_f"), zeros(MISC_BRANCH - 4), cols("nsa_branch"),
                            zeros(LANES - MISC_BRANCH - 12)], axis=-1)
    packed = jnp.concatenate([
        cols("sb_q"), cols("sb_k"), cols("sb_v"), cols("sb_gate"),
        cols("nsa_q"), cols("nsa_k_cmp"), cols("nsa_v_cmp"), cols("nsa_k_sel"), cols("nsa_v_sel"),
        cols("nsa_k_win"), cols("nsa_v_win"), misc, cols("mla_ckv"), cols("nsa_gate"),
        cols("fox_q"), cols("fox_k"), cols("fox_v"), cols("fox_gate"),
        cols("mla_cq"), _spread_rope(cols("mla_k_rope")), cols("mla_gate"),
    ], axis=-1)
    assert packed.shape[-1] == P_WIDTH
    return packed


def _pack_uq(w_uq):
    r = w_uq.shape[0]
    w = w_uq.reshape(r, N_HEADS, MLA_NOPE + MLA_ROPE)
    w = jnp.concatenate([w[..., :MLA_NOPE], _spread_rope(w[..., MLA_NOPE:])], axis=-1)
    return w.reshape(r, N_HEADS * 2 * HEAD_DIM)


def _rope_tables(pos, half, spread):
    inv_freq = ROPE_THETA ** (-jnp.arange(half, dtype=F32) / half)
    ang = pos.astype(F32)[:, None] * inv_freq[None, :]
    cos, sin = jnp.cos(ang), jnp.sin(ang)
    if spread:
        z = jnp.zeros_like(cos)
        return jnp.concatenate([cos, z, cos, z], axis=1), jnp.concatenate([-sin, z, sin, z], axis=1)
    return jnp.concatenate([cos, cos], axis=1), jnp.concatenate([-sin, sin], axis=1)


def _layer(x2, batch, seq, pre_g, post_g, w_in, b_in, w_out, forget_bias,
           pos_k, w1_k, w2_k, pos_v, w1_v, w2_v, q_norm_g, w_uq, kv_norm_g, w_ukv, tables):
    d_model = x2.shape[1]
    cos_t, sin_t, cos_c, sin_c, cos_m, sin_m = tables

    w_p = _pack_in_projection(w_in).astype(BF16)
    b_p = _pack_in_projection(b_in[None, :])
    p = norm_matmul(x2, 0, d_model, pre_g[None, :], w_p, b_p, tm=1024, tn=512, out_dtype=F32)

    o_sb = sb_attention(p, batch, seq)

    ng = seq // CMP_STRIDE
    width = CMP_STRIDE * HEAD_DIM
    t2k = p[:, C_NSA_KC:C_NSA_KC + HEAD_DIM].reshape(batch, ng, width)
    t2v = p[:, C_NSA_VC:C_NSA_VC + HEAD_DIM].reshape(batch, ng, width)
    kc, vc = nsa_compress(t2k, t2v, pos_k.reshape(2, width), pos_v.reshape(2, width),
                          w1_k.astype(BF16), w1_v.astype(BF16), w2_k.astype(BF16), w2_v.astype(BF16),
                          cos_c, sin_c, batch)
    o_nsa = nsa_attention(p, kc, vc, cos_t, sin_t, batch, seq)

    bias_row = jnp.zeros((1, LANES), F32).at[0, MISC_FOX:MISC_FOX + N_HEADS].set(forget_bias)
    ccol, crow = fox_prep(p, bias_row, batch, seq)
    o_fox = fox_attention(p, ccol, crow, batch, seq)

    qp = norm_matmul(p, C_MLA_CQ // MLA_Q_RANK, MLA_Q_RANK, q_norm_g[None, :], _pack_uq(w_uq).astype(BF16),
                     jnp.zeros((1, N_HEADS * 2 * HEAD_DIM), F32), tm=1024, tn=1024, out_dtype=F32)
    kvp = norm_matmul(p, C_MLA_CKV // MLA_KV_RANK, MLA_KV_RANK, kv_norm_g[None, :], w_ukv.astype(BF16),
                      jnp.zeros((1, N_HEADS * 2 * HEAD_DIM), F32), tm=1024, tn=1024, out_dtype=BF16)
    o_mla = mla_attention(p, qp, kvp, cos_m, sin_m, batch, seq)

    w_o = w_out.astype(BF16).reshape(4, GROUP, d_model)
    return out_projection((o_sb, o_nsa, o_fox, o_mla), w_o, post_g[None, :], x2)


def kernel(x, pre_norm_g, post_norm_g, w_in, b_in, w_out, fox_forget_bias, nsa_cmp_pos_k, nsa_cmp_w1_k, nsa_cmp_w2_k, nsa_cmp_pos_v, nsa_cmp_w1_v, nsa_cmp_w2_v, mla_q_norm_g, mla_w_uq, mla_kv_norm_g, mla_w_ukv):
    batch, seq, d_model = x.shape
    depth = w_in.shape[0]
    pos = jnp.arange(seq)
    cmp_end = jnp.arange(seq // CMP_STRIDE) * CMP_STRIDE + (CMP_LEN - 1)
    tables = (_rope_tables(pos, HEAD_DIM // 2, False) + _rope_tables(cmp_end, HEAD_DIM // 2, False)
              + _rope_tables(pos, MLA_ROPE // 2, True))
    x2 = x.reshape(batch * seq, d_model)
    for l in range(depth):
        x2 = _layer(x2, batch, seq, pre_norm_g[l], post_norm_g[l], w_in[l], b_in[l], w_out[l],
                    fox_forget_bias[l], nsa_cmp_pos_k[l], nsa_cmp_w1_k[l], nsa_cmp_w2_k[l],
                    nsa_cmp_pos_v[l], nsa_cmp_w1_v[l], nsa_cmp_w2_v[l],
                    mla_q_norm_g[l], mla_w_uq[l], mla_kv_norm_g[l], mla_w_ukv[l], tables)
    return x2.reshape(batch, seq, d_model)
```

```python
import functools

import numpy as np
import jax
import jax.numpy as jnp
from jax import lax
from jax.experimental import pallas as pl
from jax.experimental.pallas import tpu as pltpu

F32 = jnp.float32
BF16 = jnp.bfloat16

LANES = 128
HEAD_DIM = 128
N_HEADS = 4
GROUP = N_HEADS * HEAD_DIM
RMS_EPS = 1e-6
NEG_INF = -1e30
ROPE_THETA = 10000.0

CMP_LEN = 32
CMP_STRIDE = 16
SEL_LEN = 64
SEL_TOPN = 16
WINDOW = 512
FORCED_BONUS = 1e6

MLA_Q_RANK = 384
MLA_KV_RANK = 128
MLA_NOPE = 128
MLA_ROPE = 64

VMEM_LIMIT = 56 * 1024 * 1024

C_SB_Q, C_SB_K, C_SB_V, C_SB_G = 0, 512, 1024, 1536
C_NSA_Q = 2048
C_NSA_KC, C_NSA_VC, C_NSA_KS, C_NSA_VS, C_NSA_KW, C_NSA_VW = 2560, 2688, 2816, 2944, 3072, 3200
C_MISC = 3328
C_MLA_CKV = 3456
C_NSA_G = 3584
C_FOX_Q, C_FOX_K, C_FOX_V, C_FOX_G = 4096, 4608, 5120, 5632
C_MLA_CQ = 6144
C_MLA_KR = 6528
C_MLA_G = 6656
P_WIDTH = 7168
assert all(c % GROUP == 0 for c in (C_SB_Q, C_SB_K, C_SB_V, C_SB_G, C_NSA_Q, C_NSA_G, C_FOX_Q, C_FOX_K,
                                    C_FOX_V, C_FOX_G, C_MLA_G)) and C_MLA_CQ % MLA_Q_RANK == 0
MISC_FOX = 0
MISC_BRANCH = 8

_ORIG = {}
_off = 0
for _name, _w in (("sb_q", 512), ("sb_k", 512), ("sb_v", 512), ("sb_gate", 512),
                  ("nsa_q", 512), ("nsa_k_cmp", 128), ("nsa_v_cmp", 128), ("nsa_k_sel", 128),
                  ("nsa_v_sel", 128), ("nsa_k_win", 128), ("nsa_v_win", 128), ("nsa_branch", 12),
                  ("nsa_gate", 512), ("fox_q", 512), ("fox_k", 512), ("fox_v", 512), ("fox_f", 4),
                  ("fox_gate", 512), ("mla_cq", 384), ("mla_ckv", 128), ("mla_k_rope", 64),
                  ("mla_gate", 512)):
    _ORIG[_name] = (_off, _w)
    _off += _w
IN_WIDTH = _off


def _params(*sem):
    return pltpu.CompilerParams(dimension_semantics=sem, vmem_limit_bytes=VMEM_LIMIT)


def _dot(a, b):
    return jnp.dot(a, b, preferred_element_type=F32)


def _split2(x):
    hi = x.astype(BF16)
    lo = (x - hi.astype(F32)).astype(BF16)
    return hi, lo


def _split3(x):
    hi = x.astype(BF16)
    r = x - hi.astype(F32)
    mid = r.astype(BF16)
    lo = (r - mid.astype(F32)).astype(BF16)
    return hi, mid, lo


def _rope(x, cos, sin_signed):
    return x * cos + pltpu.roll(x, 64, 1) * sin_signed


def _silu(g):
    return g / (1.0 + jnp.exp(-g))


def _sigmoid(g):
    return 1.0 / (1.0 + jnp.exp(-g))


def _head(h):
    return slice(h * HEAD_DIM, (h + 1) * HEAD_DIM)


def _cast_rows(dst_ref, src_ref, chunk):
    def body(c, _):
        rows = pl.ds(pl.multiple_of(c * chunk, chunk), chunk)
        dst_ref[rows, :] = src_ref[rows, :].astype(dst_ref.dtype)
        return 0
    lax.fori_loop(0, src_ref.shape[0] // chunk, body, 0)


def _norm_matmul_kernel(x_ref, g_ref, w_ref, b_ref, o_ref, h_ref, *, chunk):
    tm = x_ref.shape[0]

    @pl.when(pl.program_id(1) == 0)
    def _():
        def body(c, _):
            rows = pl.ds(pl.multiple_of(c * chunk, chunk), chunk)
            x = x_ref[rows, :]
            ms = jnp.mean(x * x, axis=-1, keepdims=True)
            h_ref[rows, :] = (x * lax.rsqrt(ms + RMS_EPS) * g_ref[...]).astype(BF16)
            return 0
        lax.fori_loop(0, tm // chunk, body, 0)

    o_ref[...] = (_dot(h_ref[...], w_ref[...]) + b_ref[...]).astype(o_ref.dtype)


def norm_matmul(x, xcol, kdim, g, w, b, *, tm, tn, out_dtype):
    m = x.shape[0]
    n = w.shape[1]
    chunk = min(tm, 256)
    return pl.pallas_call(
        functools.partial(_norm_matmul_kernel, chunk=chunk),
        grid=(m // tm, n // tn),
        in_specs=[
            pl.BlockSpec((tm, kdim), lambda i, j: (i, xcol)),
            pl.BlockSpec((1, kdim), lambda i, j: (0, 0)),
            pl.BlockSpec((kdim, tn), lambda i, j: (0, j)),
            pl.BlockSpec((1, tn), lambda i, j: (0, j)),
        ],
        out_specs=pl.BlockSpec((tm, tn), lambda i, j: (i, j)),
        out_shape=jax.ShapeDtypeStruct((m, n), out_dtype),
        scratch_shapes=[pltpu.VMEM((tm, kdim), BF16)],
        compiler_params=_params("parallel", "arbitrary"),
        name="norm_matmul",
    )(x, g, w, b)


def _softmax_probs_t(s, m, l):
    m_new = jnp.maximum(m, jnp.max(s, axis=0, keepdims=True))
    alpha = jnp.exp(m - m_new)
    p = jnp.exp(s - m_new)
    l = alpha * l + jnp.sum(p, axis=0, keepdims=True)
    return m_new, l, alpha, p.astype(BF16)


def _softmax_step_t(s, vt, m, l, acc):
    m, l, alpha, p = _softmax_probs_t(s, m, l)
    return m, l, alpha * acc + _dot(vt, p)


def _softmax_heads_t(scores, vts, state):
    probs = [_softmax_probs_t(s, st[0], st[1]) for s, st in zip(scores, state)]
    return tuple((m, l, alpha * st[2] + _dot(vt, p)) for (m, l, alpha, p), vt, st in zip(probs, vts, state))


def _init_state_t(n, nh):
    return tuple((jnp.full((1, n), NEG_INF, F32), jnp.zeros((1, n), F32), jnp.zeros((HEAD_DIM, n), F32))
                 for _ in range(nh))


def _transpose_blocks(vt_ref, v_ref, tk):
    def body(c, _):
        x = v_ref[pl.ds(pl.multiple_of(c * tk, tk), tk), :].astype(F32)
        for h in range(x.shape[1] // HEAD_DIM):
            vt_ref[c, _head(h), :] = x[:, _head(h)].T.astype(vt_ref.dtype)
        return 0
    lax.fori_loop(0, v_ref.shape[0] // tk, body, 0)


def _sb_kernel(q_ref, k_ref, v_ref, g_ref, ut_ref, o_ref, kb_ref, vt_ref, *, tq, tk, scale):
    i = pl.program_id(1)
    nh = N_HEADS
    nsub = tq // tk

    @pl.when(i == 0)
    def _():
        _cast_rows(kb_ref, k_ref, 256)
        _transpose_blocks(vt_ref, v_ref, tq)

    ut = ut_ref[...]
    qts = [(q_ref[:, _head(h)] * scale).T.astype(BF16) for h in range(nh)]
    krow = lax.broadcasted_iota(jnp.int32, (tk, tq), 0)
    qcol = lax.broadcasted_iota(jnp.int32, (tk, tq), 1)

    def group(gj, carries, accs, masked):
        goff = pl.multiple_of(gj * tq, tq)
        pairs = [(h, c) for h in range(nh) for c in reversed(range(nsub))]
        zs = {hc: _dot(kb_ref[pl.ds(goff + hc[1] * tk, tk), _head(hc[0])], qts[hc[0]]) for hc in pairs}
        log_beta, log_keep, log_after, masks = {}, {}, {}, {}
        for hc in pairs:
            z = zs[hc]
            softplus = jnp.maximum(z, 0.0) + jnp.log(1.0 + jnp.exp(-jnp.abs(z)))
            lk = -softplus
            log_beta[hc] = z - softplus
            if masked:
                masks[hc] = (hc[1] * tk + krow) < qcol
                lk = jnp.where(masks[hc], lk, 0.0)
            log_keep[hc] = lk
            hi, lo = _split2(lk)
            r = _dot(ut, jnp.concatenate([hi, lo], axis=1))
            log_after[hc] = r[:, :tq] + r[:, tq:]
        new_carries, ws = [], []
        for h in range(nh):
            carry = carries[h]
            w_h = [None] * nsub
            for c in reversed(range(nsub)):
                w = jnp.exp(log_beta[h, c] + log_after[h, c] + carry)
                if masked:
                    w = jnp.where(masks[h, c], w, 0.0)
                w_h[c] = w.astype(BF16)
                carry = carry + jnp.sum(log_keep[h, c], axis=0, keepdims=True)
            ws.append(jnp.concatenate(w_h, axis=0))
            new_carries.append(carry)
        new_accs = [accs[h] + _dot(vt_ref[gj, _head(h), :], ws[h]) for h in range(nh)]
        return tuple(new_carries), tuple(new_accs)

    carries = tuple(jnp.zeros((1, tq), F32) for _ in range(nh))
    accs = tuple(jnp.zeros((HEAD_DIM, tq), F32) for _ in range(nh))
    carries, accs = group(i, carries, accs, True)
    carries, accs = lax.fori_loop(0, i, lambda t, c: group(i - 1 - t, c[0], c[1], False), (carries, accs))
    for h in range(nh):
        o_ref[:, _head(h)] = (accs[h].T * _silu(g_ref[:, _head(h)])).astype(o_ref.dtype)


def sb_attention(p, batch, seq, *, tq=256, tk=128):
    nq = seq // tq
    ut = jnp.asarray(np.triu(np.ones((tk, tk), np.float32), 1), BF16)
    return pl.pallas_call(
        functools.partial(_sb_kernel, tq=tq, tk=tk, scale=HEAD_DIM ** -0.5),
        grid=(batch, nq),
        in_specs=[
            pl.BlockSpec((tq, GROUP), lambda b, i: (b * nq + i, C_SB_Q // GROUP)),
            pl.BlockSpec((seq, GROUP), lambda b, i: (b, C_SB_K // GROUP)),
            pl.BlockSpec((seq, GROUP), lambda b, i: (b, C_SB_V // GROUP)),
            pl.BlockSpec((tq, GROUP), lambda b, i: (b * nq + i, C_SB_G // GROUP)),
            pl.BlockSpec((tk, tk), lambda b, i: (0, 0)),
        ],
        out_specs=pl.BlockSpec((tq, GROUP), lambda b, i: (b * nq + i, 0)),
        out_shape=jax.ShapeDtypeStruct((batch * seq, GROUP), BF16),
        scratch_shapes=[pltpu.VMEM((seq, GROUP), BF16), pltpu.VMEM((nq, GROUP, tq), BF16)],
        compiler_params=_params("parallel", "arbitrary"),
        name="sb_attention",
    )(p, p, p, p, ut)


def _fox_prep_kernel(f_ref, bias_ref, col_ref, row_ref, *, chunk):
    seq = f_ref.shape[0]
    r = lax.broadcasted_iota(jnp.int32, (chunk, chunk), 0)
    c = lax.broadcasted_iota(jnp.int32, (chunk, chunk), 1)
    tri = jnp.where(c <= r, 1.0, 0.0).astype(BF16)
    carry = jnp.zeros((1, LANES), F32)
    for n in range(seq // chunk):
        x = f_ref[n * chunk:(n + 1) * chunk, :] + bias_ref[...]
        logf = jnp.minimum(x, 0.0) - jnp.log(1.0 + jnp.exp(-jnp.abs(x)))
        hi, mid, lo = _split3(logf)
        within = _dot(tri, hi) + _dot(tri, mid) + _dot(tri, lo)
        cum = within + carry
        col_ref[n * chunk:(n + 1) * chunk, :] = cum
        row_ref[:, n * chunk:(n + 1) * chunk] = cum.T[0:8, :]
        carry = cum[chunk - 1:chunk, :]


def fox_prep(p, bias_row, batch, seq):
    chunk = LANES
    return pl.pallas_call(
        functools.partial(_fox_prep_kernel, chunk=chunk),
        grid=(batch,),
        in_specs=[
            pl.BlockSpec((seq, LANES), lambda b: (b, C_MISC // LANES)),
            pl.BlockSpec((1, LANES), lambda b: (0, 0)),
        ],
        out_specs=[
            pl.BlockSpec((seq, LANES), lambda b: (b, 0)),
            pl.BlockSpec((None, 8, seq), lambda b: (b, 0, 0)),
        ],
        out_shape=[jax.ShapeDtypeStruct((batch * seq, LANES), F32),
                   jax.ShapeDtypeStruct((batch, 8, seq), F32)],
        compiler_params=_params("parallel"),
        name="fox_prep",
    )(p, bias_row)


def _fox_kernel(q_ref, k_ref, v_ref, g_ref, ccol_ref, crow_ref, o_ref, kb_ref, vt_ref, *, tq, tk, scale):
    i = pl.program_id(1)
    nh = N_HEADS

    @pl.when(i == 0)
    def _():
        _cast_rows(kb_ref, k_ref, 256)
        _transpose_blocks(vt_ref, v_ref, tk)

    qts = [(q_ref[:, _head(h)] * scale).T.astype(BF16) for h in range(nh)]
    qoff = pl.multiple_of(i * tq, tq)
    cqs = [crow_ref[h:h + 1, pl.ds(qoff, tq)] for h in range(nh)]
    krow = lax.broadcasted_iota(jnp.int32, (tk, tq), 0)
    qcol = lax.broadcasted_iota(jnp.int32, (tk, tq), 1)

    def block(j, state, masked):
        off = pl.multiple_of(j * tk, tk)
        qk = [_dot(kb_ref[pl.ds(off, tk), _head(h)], qts[h]) for h in range(nh)]
        scores = []
        for h in range(nh):
            ck = ccol_ref[pl.ds(off, tk), h:h + 1]
            s = qk[h] + cqs[h] - ck
            if masked:
                s = jnp.where(krow <= qcol, s, NEG_INF)
            scores.append(s)
        return _softmax_heads_t(scores, [vt_ref[j, _head(h), :] for h in range(nh)], state)

    state = block(i, _init_state_t(tq, nh), True)
    state = lax.fori_loop(0, i, lambda j, st: block(j, st, False), state)
    for h in range(nh):
        _, l, acc = state[h]
        o_ref[:, _head(h)] = ((acc / l).T * _silu(g_ref[:, _head(h)])).astype(o_ref.dtype)


def fox_attention(p, ccol, crow, batch, seq, *, tq=256):
    nq = seq // tq
    return pl.pallas_call(
        functools.partial(_fox_kernel, tq=tq, tk=tq, scale=HEAD_DIM ** -0.5),
        grid=(batch, nq),
        in_specs=[
            pl.BlockSpec((tq, GROUP), lambda b, i: (b * nq + i, C_FOX_Q // GROUP)),
            pl.BlockSpec((seq, GROUP), lambda b, i: (b, C_FOX_K // GROUP)),
            pl.BlockSpec((seq, GROUP), lambda b, i: (b, C_FOX_V // GROUP)),
            pl.BlockSpec((tq, GROUP), lambda b, i: (b * nq + i, C_FOX_G // GROUP)),
            pl.BlockSpec((seq, LANES), lambda b, i: (b, 0)),
            pl.BlockSpec((None, 8, seq), lambda b, i: (b, 0, 0)),
        ],
        out_specs=pl.BlockSpec((tq, GROUP), lambda b, i: (b * nq + i, 0)),
        out_shape=jax.ShapeDtypeStruct((batch * seq, GROUP), BF16),
        scratch_shapes=[pltpu.VMEM((seq, GROUP), BF16), pltpu.VMEM((nq, GROUP, tq), BF16)],
        compiler_params=_params("parallel", "arbitrary"),
        name="fox_attention",
    )(p, p, p, p, ccol, crow)


def _mla_kernel(q_ref, kv_ref, kr_ref, g_ref, cosq_ref, sinq_ref, cos_ref, sin_ref,
                o_ref, kb_ref, vt_ref, *, tq, tk, scale):
    i = pl.program_id(1)
    nh = N_HEADS
    hw = 2 * HEAD_DIM
    seq = kv_ref.shape[0]

    @pl.when(i == 0)
    def _():
        def body(c, _):
            rows = pl.ds(pl.multiple_of(c * tk, tk), tk)
            kr = _rope(kr_ref[rows, :], cos_ref[rows, :], sin_ref[rows, :]).astype(BF16)
            for h in range(nh):
                kb_ref[rows, h * hw:h * hw + HEAD_DIM] = kv_ref[rows, h * hw:h * hw + HEAD_DIM]
                kb_ref[rows, h * hw + HEAD_DIM:(h + 1) * hw] = kr
                v = kv_ref[rows, h * hw + HEAD_DIM:(h + 1) * hw].astype(F32)
                vt_ref[c, _head(h), :] = v.T.astype(BF16)
            return 0
        lax.fori_loop(0, seq // tk, body, 0)

    cosq = cosq_ref[...]
    sinq = sinq_ref[...]
    qts = []
    for h in range(nh):
        qn = q_ref[:, h * hw:h * hw + HEAD_DIM] * scale
        qr = _rope(q_ref[:, h * hw + HEAD_DIM:(h + 1) * hw], cosq, sinq) * scale
        qts.append(jnp.concatenate([qn.T, qr.T], axis=0).astype(BF16))
    krow = lax.broadcasted_iota(jnp.int32, (tk, tq), 0)
    qcol = lax.broadcasted_iota(jnp.int32, (tk, tq), 1)

    def block(j, state, masked):
        off = pl.multiple_of(j * tk, tk)
        scores = [_dot(kb_ref[pl.ds(off, tk), h * hw:(h + 1) * hw], qts[h]) for h in range(nh)]
        if masked:
            scores = [jnp.where(krow <= qcol, s, NEG_INF) for s in scores]
        return _softmax_heads_t(scores, [vt_ref[j, _head(h), :] for h in range(nh)], state)

    state = block(i, _init_state_t(tq, nh), True)
    state = lax.fori_loop(0, i, lambda j, st: block(j, st, False), state)
    for h in range(nh):
        _, l, acc = state[h]
        o_ref[:, _head(h)] = ((acc / l).T * _silu(g_ref[:, _head(h)])).astype(o_ref.dtype)


def mla_attention(p, qp, kvp, cos_m, sin_m, batch, seq, *, tq=256):
    nq = seq // tq
    hb = HEAD_DIM
    wide = N_HEADS * 2 * hb
    return pl.pallas_call(
        functools.partial(_mla_kernel, tq=tq, tk=tq, scale=(MLA_NOPE + MLA_ROPE) ** -0.5),
        grid=(batch, nq),
        in_specs=[
            pl.BlockSpec((tq, wide), lambda b, i: (b * nq + i, 0)),
            pl.BlockSpec((seq, wide), lambda b, i: (b, 0)),
            pl.BlockSpec((seq, hb), lambda b, i: (b, C_MLA_KR // hb)),
            pl.BlockSpec((tq, GROUP), lambda b, i: (b * nq + i, C_MLA_G // GROUP)),
            pl.BlockSpec((tq, hb), lambda b, i: (i, 0)),
            pl.BlockSpec((tq, hb), lambda b, i: (i, 0)),
            pl.BlockSpec((seq, hb), lambda b, i: (0, 0)),
            pl.BlockSpec((seq, hb), lambda b, i: (0, 0)),
        ],
        out_specs=pl.BlockSpec((tq, GROUP), lambda b, i: (b * nq + i, 0)),
        out_shape=jax.ShapeDtypeStruct((batch * seq, GROUP), BF16),
        scratch_shapes=[pltpu.VMEM((seq, wide), BF16), pltpu.VMEM((nq, GROUP, tq), BF16)],
        compiler_params=_params("parallel", "arbitrary"),
        name="mla_attention",
    )(qp, kvp, p, p, cos_m, sin_m, cos_m, sin_m)


def _compress_kernel(tk_ref, tv_ref, posk_ref, posv_ref, w1k_ref, w1v_ref, w2k_ref, w2v_ref,
                     cos_ref, sin_ref, kc_ref, vc_ref):
    half = tk_ref.shape[1]

    def mlp(t_ref, pos_ref, w1_ref, w2_ref):
        t = t_ref[...]
        a = _dot((t + pos_ref[0:1, :]).astype(BF16), w1_ref[0:half, :])
        bm = _dot((t + pos_ref[1:2, :]).astype(BF16), w1_ref[half:2 * half, :])
        hidden = a + pltpu.roll(bm, bm.shape[0] - 1, 0)
        return _dot(_silu(hidden).astype(BF16), w2_ref[...])

    kc = mlp(tk_ref, posk_ref, w1k_ref, w2k_ref)
    kc_ref[...] = _rope(kc, cos_ref[...], sin_ref[...]).astype(BF16)
    vc_ref[...] = mlp(tv_ref, posv_ref, w1v_ref, w2v_ref).astype(BF16)


def nsa_compress(t2k, t2v, posk, posv, w1k, w1v, w2k, w2v, cos_c, sin_c, batch):
    ng, width = t2k.shape[1], t2k.shape[2]
    full = lambda a: pl.BlockSpec(a.shape, lambda b: (0,) * a.ndim)
    return pl.pallas_call(
        _compress_kernel,
        grid=(batch,),
        in_specs=[
            pl.BlockSpec((None, ng, width), lambda b: (b, 0, 0)),
            pl.BlockSpec((None, ng, width), lambda b: (b, 0, 0)),
            full(posk), full(posv), full(w1k), full(w1v), full(w2k), full(w2v), full(cos_c), full(sin_c),
        ],
        out_specs=[pl.BlockSpec((None, ng, HEAD_DIM), lambda b: (b, 0, 0)),
                   pl.BlockSpec((None, ng, HEAD_DIM), lambda b: (b, 0, 0))],
        out_shape=[jax.ShapeDtypeStruct((batch, ng, HEAD_DIM), BF16),
                   jax.ShapeDtypeStruct((batch, ng, HEAD_DIM), BF16)],
        compiler_params=_params("parallel"),
        name="nsa_compress",
    )(t2k, t2v, posk, posv, w1k, w1v, w2k, w2v, cos_c, sin_c)


def _nsa_kernel(q_ref, ks_ref, vs_ref, kw_ref, vw_ref, kc_ref, vc_ref, misc_ref, g_ref,
                cosq_ref, sinq_ref, cos_ref, sin_ref, c2s_ref, et_ref, o_ref,
                ksb_ref, vst_ref, kwb_ref, vwt_ref, vct_ref, *, tq, scale, n_sel):
    i = pl.program_id(1)
    seq = ks_ref.shape[0]
    tk = tq
    nh = N_HEADS
    n_all = nh * tq

    @pl.when(i == 0)
    def _():
        vct_ref[...] = vc_ref[...].astype(F32).T.astype(BF16)

        def body(c, _):
            rows = pl.ds(pl.multiple_of(c * tk, tk), tk)
            cos = cos_ref[rows, :]
            sin = sin_ref[rows, :]
            ksb_ref[rows, :] = _rope(ks_ref[rows, :], cos, sin).astype(BF16)
            kwb_ref[rows, :] = _rope(kw_ref[rows, :], cos, sin).astype(BF16)
            vst_ref[c] = vs_ref[rows, :].T.astype(BF16)
            vwt_ref[c] = vw_ref[rows, :].T.astype(BF16)
            return 0
        lax.fori_loop(0, seq // tk, body, 0)

    cosq = cosq_ref[...]
    sinq = sinq_ref[...]
    qt = jnp.concatenate([(_rope(q_ref[:, _head(h)], cosq, sinq) * scale).T.astype(BF16)
                          for h in range(nh)], axis=1)

    def heads(x):
        return jnp.concatenate([x] * nh, axis=1)

    n_row = lax.broadcasted_iota(jnp.int32, (LANES, n_all), 0)
    qpos_c = i * tq + (lax.broadcasted_iota(jnp.int32, (LANES, n_all), 1) & (tq - 1))
    zc = _dot(kc_ref[...], qt)
    mask_c = (n_row * CMP_STRIDE + (CMP_LEN - 1)) <= qpos_c
    mc = jnp.max(jnp.where(mask_c, zc, NEG_INF), axis=0, keepdims=True)
    pc = jnp.where(mask_c, jnp.exp(zc - mc), 0.0)
    lc = jnp.sum(pc, axis=0, keepdims=True)
    pc = pc / jnp.where(lc > 0.0, lc, 1.0)
    o_cmp = _dot(vct_ref[...], pc.astype(BF16))

    pc_sum = pc[:, 0:tq]
    for h in range(1, nh):
        pc_sum = pc_sum + pc[:, h * tq:(h + 1) * tq]
    hi, lo = _split2(pc_sum)
    imp = _dot(c2s_ref[...], hi) + _dot(c2s_ref[...], lo)
    blk = lax.broadcasted_iota(jnp.int32, (n_sel, tq), 0)
    cur = (i * tq + lax.broadcasted_iota(jnp.int32, (n_sel, tq), 1)) >> 6
    valid = blk <= cur
    forced = (blk == 0) | (blk == cur) | (blk == cur - 1)
    score = jnp.where(valid, jnp.where(forced, FORCED_BONUS, imp), NEG_INF)
    rank = jnp.zeros((n_sel, tq), F32)
    for k in range(n_sel):
        sk = score[k:k + 1, :]
        later = jnp.where(blk > k, 1.0, 0.0)
        rank = rank + jnp.where(sk > score, 1.0, 0.0) + jnp.where(sk == score, later, 0.0)
    sel_t = jnp.where(rank < float(SEL_TOPN), 1.0, 0.0)
    sel_t = jnp.concatenate([sel_t, jnp.zeros((LANES - n_sel, tq), F32)], axis=0).astype(BF16)

    krow = lax.broadcasted_iota(jnp.int32, (tk, n_all), 0)
    qcol = lax.broadcasted_iota(jnp.int32, (tk, n_all), 1) & (tq - 1)
    causal = krow <= qcol

    state0 = _init_state_t(n_all, 1)[0]

    def sel_block(j, st, diag):
        off = pl.multiple_of(j * tk, tk)
        s = _dot(ksb_ref[pl.ds(off, tk), :], qt)
        chosen = heads(_dot(et_ref[j], sel_t)) > 0.5
        s = jnp.where(chosen, s, NEG_INF)
        if diag:
            s = jnp.where(causal, s, NEG_INF)
        return _softmax_step_t(s, vst_ref[j], *st)

    st = lax.fori_loop(0, i, lambda j, s: sel_block(j, s, False), state0)
    _, ls, accs = sel_block(i, st, True)
    o_slc = accs / ls

    st = state0
    nwin = WINDOW // tk
    for d in range(nwin + 1):
        jj = i - d
        jc = jnp.maximum(jj, 0)
        s = _dot(kwb_ref[pl.ds(pl.multiple_of(jc * tk, tk), tk), :], qt)
        if d == 0:
            s = jnp.where(causal, s, NEG_INF)
        else:
            s = jnp.where(jj >= 0, s, NEG_INF)
            if d == nwin:
                s = jnp.where(krow > qcol, s, NEG_INF)
        st = _softmax_step_t(s, vwt_ref[jc], *st)
    o_win = st[2] / st[1]

    gates_t = _sigmoid(misc_ref[...]).T
    for h in range(nh):
        cols = slice(h * tq, (h + 1) * tq)
        c0 = MISC_BRANCH + 3 * h
        o = (gates_t[c0:c0 + 1, :] * o_cmp[:, cols] + gates_t[c0 + 1:c0 + 2, :] * o_slc[:, cols]
             + gates_t[c0 + 2:c0 + 3, :] * o_win[:, cols])
        o_ref[:, _head(h)] = (o.T * _silu(g_ref[:, _head(h)])).astype(o_ref.dtype)


def nsa_attention(p, kc, vc, cos_t, sin_t, batch, seq, *, tq=256):
    nq = seq // tq
    hb = HEAD_DIM
    n_cmp = (seq - CMP_LEN) // CMP_STRIDE + 1
    n_sel = seq // SEL_LEN
    ncp = kc.shape[1]
    assert ncp == LANES and n_cmp <= ncp and n_sel <= LANES and n_sel % 16 == 0
    assert SEL_LEN == 64 and WINDOW % tq == 0 and tq % LANES == 0 and tq & (tq - 1) == 0
    cmp_start = np.arange(n_cmp) * CMP_STRIDE
    sel_start = np.arange(n_sel) * SEL_LEN
    overlap = np.clip(np.minimum(cmp_start[:, None] + CMP_LEN, sel_start[None, :] + SEL_LEN)
                      - np.maximum(cmp_start[:, None], sel_start[None, :]), 0, None)
    c2s = np.zeros((n_sel, ncp), np.float32)
    c2s[:, :n_cmp] = (overlap / CMP_LEN).T
    expand = np.zeros((nq, tq, LANES), np.float32)
    keys = np.arange(seq)
    expand[keys // tq, keys % tq, keys // SEL_LEN] = 1.0
    kv = lambda c: pl.BlockSpec((seq, hb), lambda b, i: (b, c // hb))
    return pl.pallas_call(
        functools.partial(_nsa_kernel, tq=tq, scale=HEAD_DIM ** -0.5, n_sel=n_sel),
        grid=(batch, nq),
        in_specs=[
            pl.BlockSpec((tq, GROUP), lambda b, i: (b * nq + i, C_NSA_Q // GROUP)),
            kv(C_NSA_KS), kv(C_NSA_VS), kv(C_NSA_KW), kv(C_NSA_VW),
            pl.BlockSpec((None, ncp, hb), lambda b, i: (b, 0, 0)),
            pl.BlockSpec((None, ncp, hb), lambda b, i: (b, 0, 0)),
            pl.BlockSpec((tq, LANES), lambda b, i: (b * nq + i, C_MISC // LANES)),
            pl.BlockSpec((tq, GROUP), lambda b, i: (b * nq + i, C_NSA_G // GROUP)),
            pl.BlockSpec((tq, hb), lambda b, i: (i, 0)),
            pl.BlockSpec((tq, hb), lambda b, i: (i, 0)),
            pl.BlockSpec((seq, hb), lambda b, i: (0, 0)),
            pl.BlockSpec((seq, hb), lambda b, i: (0, 0)),
            pl.BlockSpec((n_sel, ncp), lambda b, i: (0, 0)),
            pl.BlockSpec((nq, tq, LANES), lambda b, i: (0, 0, 0)),
        ],
        out_specs=pl.BlockSpec((tq, GROUP), lambda b, i: (b * nq + i, 0)),
        out_shape=jax.ShapeDtypeStruct((batch * seq, GROUP), BF16),
        scratch_shapes=[pltpu.VMEM((seq, hb), BF16), pltpu.VMEM((nq, hb, tq), BF16),
                        pltpu.VMEM((seq, hb), BF16), pltpu.VMEM((nq, hb, tq), BF16),
                        pltpu.VMEM((hb, ncp), BF16)],
        compiler_params=_params("parallel", "arbitrary"),
        name="nsa_attention",
    )(p, p, p, p, p, kc, vc, p, p, cos_t, sin_t, cos_t, sin_t,
      jnp.asarray(c2s, BF16), jnp.asarray(expand, BF16))


def _out_kernel(ma_ref, mb_ref, mc_ref, md_ref, w_ref, g_ref, x_ref, o_ref):
    y = _dot(ma_ref[...], w_ref[0])
    y = y + _dot(mb_ref[...], w_ref[1])
    y = y + _dot(mc_ref[...], w_ref[2])
    y = y + _dot(md_ref[...], w_ref[3])
    ms = jnp.mean(y * y, axis=-1, keepdims=True)
    o_ref[...] = x_ref[...] + y * lax.rsqrt(ms + RMS_EPS) * g_ref[...]


def out_projection(mixes, w, g, x, *, tm=256):
    m, d = x.shape
    mix_spec = pl.BlockSpec((tm, GROUP), lambda i: (i, 0))
    return pl.pallas_call(
        _out_kernel,
        grid=(m // tm,),
        in_specs=[mix_spec, mix_spec, mix_spec, mix_spec,
                  pl.BlockSpec((len(mixes), GROUP, d), lambda i: (0, 0, 0)),
                  pl.BlockSpec((1, d), lambda i: (0, 0)),
                  pl.BlockSpec((tm, d), lambda i: (i, 0))],
        out_specs=pl.BlockSpec((tm, d), lambda i: (i, 0)),
        out_shape=jax.ShapeDtypeStruct((m, d), F32),
        compiler_params=_params("parallel"),
        name="out_projection",
    )(*mixes, w, g, x)


def _spread_rope(w):
    z = jnp.zeros(w.shape[:-1] + (MLA_ROPE // 2,), w.dtype)
    return jnp.concatenate([w[..., :MLA_ROPE // 2], z, w[..., MLA_ROPE // 2:], z], axis=-1)


def _pack_in_projection(w):
    def cols(name):
        o, n = _ORIG[name]
        return w[..., o:o + n]

    def zeros(n):
        return jnp.zeros(w.shape[:-1] + (n,), w.dtype)

    misc = jnp.concatenate([cols("fox_f"), zeros(MISC_BRANCH - 4), cols("nsa_branch"),
                            zeros(LANES - MISC_BRANCH - 12)], axis=-1)
    packed = jnp.concatenate([
        cols("sb_q"), cols("sb_k"), cols("sb_v"), cols("sb_gate"),
        cols("nsa_q"), cols("nsa_k_cmp"), cols("nsa_v_cmp"), cols("nsa_k_sel"), cols("nsa_v_sel"),
        cols("nsa_k_win"), cols("nsa_v_win"), misc, cols("mla_ckv"), cols("nsa_gate"),
        cols("fox_q"), cols("fox_k"), cols("fox_v"), cols("fox_gate"),
        cols("mla_cq"), _spread_rope(cols("mla_k_rope")), cols("mla_gate"),
    ], axis=-1)
    assert packed.shape[-1] == P_WIDTH
    return packed


def _pack_uq(w_uq):
    r = w_uq.shape[0]
    w = w_uq.reshape(r, N_HEADS, MLA_NOPE + MLA_ROPE)
    w = jnp.concatenate([w[..., :MLA_NOPE], _spread_rope(w[..., MLA_NOPE:])], axis=-1)
    return w.reshape(r, N_HEADS * 2 * HEAD_DIM)


def _rope_tables(pos, half, spread):
    inv_freq = ROPE_THETA ** (-jnp.arange(half, dtype=F32) / half)
    ang = pos.astype(F32)[:, None] * inv_freq[None, :]
    cos, sin = jnp.cos(ang), jnp.sin(ang)
    if spread:
        z = jnp.zeros_like(cos)
        return jnp.concatenate([cos, z, cos, z], axis=1), jnp.concatenate([-sin, z, sin, z], axis=1)
    return jnp.concatenate([cos, cos], axis=1), jnp.concatenate([-sin, sin], axis=1)


def _layer(x2, batch, seq, pre_g, post_g, w_in, b_in, w_out, forget_bias,
           pos_k, w1_k, w2_k, pos_v, w1_v, w2_v, q_norm_g, w_uq, kv_norm_g, w_ukv, tables):
    d_model = x2.shape[1]
    cos_t, sin_t, cos_c, sin_c, cos_m, sin_m = tables

    w_p = _pack_in_projection(w_in.astype(BF16))
    b_p = _pack_in_projection(b_in[None, :])
    p = norm_matmul(x2, 0, d_model, pre_g[None, :], w_p, b_p, tm=1024, tn=512, out_dtype=F32)

    o_sb = sb_attention(p, batch, seq)

    ng = seq // CMP_STRIDE
    width = CMP_STRIDE * HEAD_DIM
    t2k = p[:, C_NSA_KC:C_NSA_KC + HEAD_DIM].reshape(batch, ng, width)
    t2v = p[:, C_NSA_VC:C_NSA_VC + HEAD_DIM].reshape(batch, ng, width)
    kc, vc = nsa_compress(t2k, t2v, pos_k.reshape(2, width), pos_v.reshape(2, width),
                          w1_k.astype(BF16), w1_v.astype(BF16), w2_k.astype(BF16), w2_v.astype(BF16),
                          cos_c, sin_c, batch)
    o_nsa = nsa_attention(p, kc, vc, cos_t, sin_t, batch, seq)

    bias_row = jnp.zeros((1, LANES), F32).at[0, MISC_FOX:MISC_FOX + N_HEADS].set(forget_bias)
    ccol, crow = fox_prep(p, bias_row, batch, seq)
    o_fox = fox_attention(p, ccol, crow, batch, seq)

    qp = norm_matmul(p, C_MLA_CQ // MLA_Q_RANK, MLA_Q_RANK, q_norm_g[None, :], _pack_uq(w_uq).astype(BF16),
                     jnp.zeros((1, N_HEADS * 2 * HEAD_DIM), F32), tm=1024, tn=1024, out_dtype=F32)
    kvp = norm_matmul(p, C_MLA_CKV // MLA_KV_RANK, MLA_KV_RANK, kv_norm_g[None, :], w_ukv.astype(BF16),
                      jnp.zeros((1, N_HEADS * 2 * HEAD_DIM), F32), tm=1024, tn=1024, out_dtype=BF16)
    o_mla = mla_attention(p, qp, kvp, cos_m, sin_m, batch, seq)

    w_o = w_out.astype(BF16).reshape(4, GROUP, d_model)
    return out_projection((o_sb, o_nsa, o_fox, o_mla), w_o, post_g[None, :], x2)


def kernel(x, pre_norm_g, post_norm_g, w_in, b_in, w_out, fox_forget_bias, nsa_cmp_pos_k, nsa_cmp_w1_k, nsa_cmp_w2_k, nsa_cmp_pos_v, nsa_cmp_w1_v, nsa_cmp_w2_v, mla_q_norm_g, mla_w_uq, mla_kv_norm_g, mla_w_ukv):
    batch, seq, d_model = x.shape
    depth = w_in.shape[0]
    pos = jnp.arange(seq)
    cmp_end = jnp.arange(seq // CMP_STRIDE) * CMP_STRIDE + (CMP_LEN - 1)
    tables = (_rope_tables(pos, HEAD_DIM // 2, False) + _rope_tables(cmp_end, HEAD_DIM // 2, False)
              + _rope_tables(pos, MLA_ROPE // 2, True))
    x2 = x.reshape(batch * seq, d_model)
    for l in range(depth):
        x2 = _layer(x2, batch, seq, pre_norm_g[l], post_norm_g[l], w_in[l], b_in[l], w_out[l],
                    fox_forget_bias[l], nsa_cmp_pos_k[l], nsa_cmp_w1_k[l], nsa_cmp_w2_k[l],
                    nsa_cmp_pos_v[l], nsa_cmp_w1_v[l], nsa_cmp_w2_v[l],
                    mla_q_norm_g[l], mla_w_uq[l], mla_kv_norm_g[l], mla_w_ukv[l], tables)
    return x2.reshape(batch, seq, d_model)
```

```python
import functools

import numpy as np
import jax
import jax.numpy as jnp
from jax import lax
from jax.experimental import pallas as pl
from jax.experimental.pallas import tpu as pltpu

F32 = jnp.float32
BF16 = jnp.bfloat16

LANES = 128
HEAD_DIM = 128
N_HEADS = 4
GROUP = N_HEADS * HEAD_DIM
RMS_EPS = 1e-6
NEG_INF = -1e30
ROPE_THETA = 10000.0

CMP_LEN = 32
CMP_STRIDE = 16
SEL_LEN = 64
SEL_TOPN = 16
WINDOW = 512
FORCED_BONUS = 1e6

MLA_Q_RANK = 384
MLA_KV_RANK = 128
MLA_NOPE = 128
MLA_ROPE = 64

VMEM_LIMIT = 56 * 1024 * 1024

C_SB_Q, C_SB_K, C_SB_V, C_SB_G = 0, 512, 1024, 1536
C_NSA_Q = 2048
C_NSA_KC, C_NSA_VC, C_NSA_KS, C_NSA_VS, C_NSA_KW, C_NSA_VW = 2560, 2688, 2816, 2944, 3072, 3200
C_MISC = 3328
C_MLA_CKV = 3456
C_NSA_G = 3584
C_FOX_Q, C_FOX_K, C_FOX_V, C_FOX_G = 4096, 4608, 5120, 5632
C_MLA_CQ = 6144
C_MLA_KR = 6528
C_MLA_G = 6656
P_WIDTH = 7168
assert all(c % GROUP == 0 for c in (C_SB_Q, C_SB_K, C_SB_V, C_SB_G, C_NSA_Q, C_NSA_G, C_FOX_Q, C_FOX_K,
                                    C_FOX_V, C_FOX_G, C_MLA_G)) and C_MLA_CQ % MLA_Q_RANK == 0
MISC_FOX = 0
MISC_BRANCH = 8

_ORIG = {}
_off = 0
for _name, _w in (("sb_q", 512), ("sb_k", 512), ("sb_v", 512), ("sb_gate", 512),
                  ("nsa_q", 512), ("nsa_k_cmp", 128), ("nsa_v_cmp", 128), ("nsa_k_sel", 128),
                  ("nsa_v_sel", 128), ("nsa_k_win", 128), ("nsa_v_win", 128), ("nsa_branch", 12),
                  ("nsa_gate", 512), ("fox_q", 512), ("fox_k", 512), ("fox_v", 512), ("fox_f", 4),
                  ("fox_gate", 512), ("mla_cq", 384), ("mla_ckv", 128), ("mla_k_rope", 64),
                  ("mla_gate", 512)):
    _ORIG[_name] = (_off, _w)
    _off += _w
IN_WIDTH = _off


def _params(*sem):
    return pltpu.CompilerParams(dimension_semantics=sem, vmem_limit_bytes=VMEM_LIMIT)


def _dot(a, b):
    return jnp.dot(a, b, preferred_element_type=F32)


def _split2(x):
    hi = x.astype(BF16)
    lo = (x - hi.astype(F32)).astype(BF16)
    return hi, lo


def _split3(x):
    hi = x.astype(BF16)
    r = x - hi.astype(F32)
    mid = r.astype(BF16)
    lo = (r - mid.astype(F32)).astype(BF16)
    return hi, mid, lo


def _rope(x, cos, sin_signed):
    return x * cos + pltpu.roll(x, 64, 1) * sin_signed


def _silu(g):
    return g / (1.0 + jnp.exp(-g))


def _sigmoid(g):
    return 1.0 / (1.0 + jnp.exp(-g))


def _head(h):
    return slice(h * HEAD_DIM, (h + 1) * HEAD_DIM)


def _cast_rows(dst_ref, src_ref, chunk):
    def body(c, _):
        rows = pl.ds(pl.multiple_of(c * chunk, chunk), chunk)
        dst_ref[rows, :] = src_ref[rows, :].astype(dst_ref.dtype)
        return 0
    lax.fori_loop(0, src_ref.shape[0] // chunk, body, 0)


def _norm_matmul_kernel(x_ref, g_ref, w_ref, b_ref, o_ref, h_ref, *, chunk):
    tm = x_ref.shape[0]

    @pl.when(pl.program_id(1) == 0)
    def _():
        def body(c, _):
            rows = pl.ds(pl.multiple_of(c * chunk, chunk), chunk)
            x = x_ref[rows, :]
            ms = jnp.mean(x * x, axis=-1, keepdims=True)
            h_ref[rows, :] = (x * lax.rsqrt(ms + RMS_EPS) * g_ref[...]).astype(BF16)
            return 0
        lax.fori_loop(0, tm // chunk, body, 0)

    o_ref[...] = (_dot(h_ref[...], w_ref[...]) + b_ref[...]).astype(o_ref.dtype)


def norm_matmul(x, xcol, kdim, g, w, b, *, tm, tn, out_dtype):
    m = x.shape[0]
    n = w.shape[1]
    chunk = min(tm, 256)
    return pl.pallas_call(
        functools.partial(_norm_matmul_kernel, chunk=chunk),
        grid=(m // tm, n // tn),
        in_specs=[
            pl.BlockSpec((tm, kdim), lambda i, j: (i, xcol)),
            pl.BlockSpec((1, kdim), lambda i, j: (0, 0)),
            pl.BlockSpec((kdim, tn), lambda i, j: (0, j)),
            pl.BlockSpec((1, tn), lambda i, j: (0, j)),
        ],
        out_specs=pl.BlockSpec((tm, tn), lambda i, j: (i, j)),
        out_shape=jax.ShapeDtypeStruct((m, n), out_dtype),
        scratch_shapes=[pltpu.VMEM((tm, kdim), BF16)],
        compiler_params=_params("parallel", "arbitrary"),
        name="norm_matmul",
    )(x, g, w, b)


def _softmax_probs_t(s, m, l):
    m_new = jnp.maximum(m, jnp.max(s, axis=0, keepdims=True))
    alpha = jnp.exp(m - m_new)
    p = jnp.exp(s - m_new)
    l = alpha * l + jnp.sum(p, axis=0, keepdims=True)
    return m_new, l, alpha, p.astype(BF16)


def _softmax_step_t(s, vt, m, l, acc):
    m, l, alpha, p = _softmax_probs_t(s, m, l)
    return m, l, alpha * acc + _dot(vt, p)


def _softmax_heads_t(scores, vts, state):
    probs = [_softmax_probs_t(s, st[0], st[1]) for s, st in zip(scores, state)]
    return tuple((m, l, alpha * st[2] + _dot(vt, p)) for (m, l, alpha, p), vt, st in zip(probs, vts, state))


def _init_state_t(n, nh):
    return tuple((jnp.full((1, n), NEG_INF, F32), jnp.zeros((1, n), F32), jnp.zeros((HEAD_DIM, n), F32))
                 for _ in range(nh))


def _transpose_blocks(vt_ref, v_ref, tk):
    def body(c, _):
        x = v_ref[pl.ds(pl.multiple_of(c * tk, tk), tk), :].astype(F32)
        for h in range(x.shape[1] // HEAD_DIM):
            vt_ref[c, _head(h), :] = x[:, _head(h)].T.astype(vt_ref.dtype)
        return 0
    lax.fori_loop(0, v_ref.shape[0] // tk, body, 0)


def _sb_kernel(q_ref, k_ref, v_ref, g_ref, ut_ref, o_ref, kb_ref, vt_ref, *, tq, gk, tk, scale):
    i = pl.program_id(1)
    nh = N_HEADS
    nsub = gk // tk

    @pl.when(i == 0)
    def _():
        _cast_rows(kb_ref, k_ref, 256)
        _transpose_blocks(vt_ref, v_ref, gk)

    ut = ut_ref[...]
    qts = [(q_ref[:, _head(h)] * scale).T.astype(BF16) for h in range(nh)]
    krow = lax.broadcasted_iota(jnp.int32, (tk, tq), 0)
    qcol = lax.broadcasted_iota(jnp.int32, (tk, tq), 1)

    def group(gj, carries, accs, diag):
        masked = diag is not None
        goff = pl.multiple_of(gj * gk, gk)
        pairs = [(h, c) for h in range(nh) for c in reversed(range(nsub))]
        zs = {hc: _dot(kb_ref[pl.ds(goff + hc[1] * tk, tk), _head(hc[0])], qts[hc[0]]) for hc in pairs}
        log_beta, log_keep, log_after, masks = {}, {}, {}, {}
        for hc in pairs:
            z = zs[hc]
            softplus = jnp.maximum(z, 0.0) + jnp.log(1.0 + jnp.exp(-jnp.abs(z)))
            lk = -softplus
            log_beta[hc] = z - softplus
            if masked:
                masks[hc] = (diag * gk + hc[1] * tk + krow) < qcol
                lk = jnp.where(masks[hc], lk, 0.0)
            log_keep[hc] = lk
            hi, lo = _split2(lk)
            r = _dot(ut, jnp.concatenate([hi, lo], axis=1))
            log_after[hc] = r[:, :tq] + r[:, tq:]
        new_carries, ws = [], []
        for h in range(nh):
            carry = carries[h]
            w_h = [None] * nsub
            for c in reversed(range(nsub)):
                w = jnp.exp(log_beta[h, c] + log_after[h, c] + carry)
                if masked:
                    w = jnp.where(masks[h, c], w, 0.0)
                w_h[c] = w.astype(BF16)
                carry = carry + jnp.sum(log_keep[h, c], axis=0, keepdims=True)
            ws.append(jnp.concatenate(w_h, axis=0))
            new_carries.append(carry)
        new_accs = [accs[h] + _dot(vt_ref[gj, _head(h), :], ws[h]) for h in range(nh)]
        return tuple(new_carries), tuple(new_accs)

    carries = tuple(jnp.zeros((1, tq), F32) for _ in range(nh))
    accs = tuple(jnp.zeros((HEAD_DIM, tq), F32) for _ in range(nh))
    ndiag = tq // gk
    for d in reversed(range(ndiag)):
        carries, accs = group(i * ndiag + d, carries, accs, d)
    carries, accs = lax.fori_loop(0, i * ndiag, lambda t, c: group(i * ndiag - 1 - t, c[0], c[1], None),
                                  (carries, accs))
    for h in range(nh):
        o_ref[:, _head(h)] = (accs[h].T * _silu(g_ref[:, _head(h)])).astype(o_ref.dtype)


def sb_attention(p, batch, seq, *, tq=512, gk=256, tk=128):
    nq = seq // tq
    ut = jnp.asarray(np.triu(np.ones((tk, tk), np.float32), 1), BF16)
    return pl.pallas_call(
        functools.partial(_sb_kernel, tq=tq, gk=gk, tk=tk, scale=HEAD_DIM ** -0.5),
        grid=(batch, nq),
        in_specs=[
            pl.BlockSpec((tq, GROUP), lambda b, i: (b * nq + i, C_SB_Q // GROUP)),
            pl.BlockSpec((seq, GROUP), lambda b, i: (b, C_SB_K // GROUP)),
            pl.BlockSpec((seq, GROUP), lambda b, i: (b, C_SB_V // GROUP)),
            pl.BlockSpec((tq, GROUP), lambda b, i: (b * nq + i, C_SB_G // GROUP)),
            pl.BlockSpec((tk, tk), lambda b, i: (0, 0)),
        ],
        out_specs=pl.BlockSpec((tq, GROUP), lambda b, i: (b * nq + i, 0)),
        out_shape=jax.ShapeDtypeStruct((batch * seq, GROUP), BF16),
        scratch_shapes=[pltpu.VMEM((seq, GROUP), BF16), pltpu.VMEM((seq // gk, GROUP, gk), BF16)],
        compiler_params=_params("parallel", "arbitrary"),
        name="sb_attention",
    )(p, p, p, p, ut)


def _fox_prep_kernel(f_ref, bias_ref, col_ref, row_ref, *, chunk):
    seq = f_ref.shape[0]
    r = lax.broadcasted_iota(jnp.int32, (chunk, chunk), 0)
    c = lax.broadcasted_iota(jnp.int32, (chunk, chunk), 1)
    tri = jnp.where(c <= r, 1.0, 0.0).astype(BF16)
    carry = jnp.zeros((1, LANES), F32)
    for n in range(seq // chunk):
        x = f_ref[n * chunk:(n + 1) * chunk, :] + bias_ref[...]
        logf = jnp.minimum(x, 0.0) - jnp.log(1.0 + jnp.exp(-jnp.abs(x)))
        hi, mid, lo = _split3(logf)
        within = _dot(tri, hi) + _dot(tri, mid) + _dot(tri, lo)
        cum = within + carry
        col_ref[n * chunk:(n + 1) * chunk, :] = cum
        row_ref[:, n * chunk:(n + 1) * chunk] = cum.T[0:8, :]
        carry = cum[chunk - 1:chunk, :]


def fox_prep(p, bias_row, batch, seq):
    chunk = LANES
    return pl.pallas_call(
        functools.partial(_fox_prep_kernel, chunk=chunk),
        grid=(batch,),
        in_specs=[
            pl.BlockSpec((seq, LANES), lambda b: (b, C_MISC // LANES)),
            pl.BlockSpec((1, LANES), lambda b: (0, 0)),
        ],
        out_specs=[
            pl.BlockSpec((seq, LANES), lambda b: (b, 0)),
            pl.BlockSpec((None, 8, seq), lambda b: (b, 0, 0)),
        ],
        out_shape=[jax.ShapeDtypeStruct((batch * seq, LANES), F32),
                   jax.ShapeDtypeStruct((batch, 8, seq), F32)],
        compiler_params=_params("parallel"),
        name="fox_prep",
    )(p, bias_row)


def _fox_kernel(q_ref, k_ref, v_ref, g_ref, ccol_ref, crow_ref, o_ref, kb_ref, vt_ref, *, tq, tk, scale):
    i = pl.program_id(1)
    nh = N_HEADS

    @pl.when(i == 0)
    def _():
        _cast_rows(kb_ref, k_ref, 256)
        _transpose_blocks(vt_ref, v_ref, tk)

    qts = [(q_ref[:, _head(h)] * scale).T.astype(BF16) for h in range(nh)]
    qoff = pl.multiple_of(i * tq, tq)
    cqs = [crow_ref[h:h + 1, pl.ds(qoff, tq)] for h in range(nh)]
    krow = lax.broadcasted_iota(jnp.int32, (tk, tq), 0)
    qcol = lax.broadcasted_iota(jnp.int32, (tk, tq), 1)

    def block(j, state, diag):
        off = pl.multiple_of(j * tk, tk)
        qk = [_dot(kb_ref[pl.ds(off, tk), _head(h)], qts[h]) for h in range(nh)]
        scores = []
        for h in range(nh):
            ck = ccol_ref[pl.ds(off, tk), h:h + 1]
            s = qk[h] + cqs[h] - ck
            if diag is not None:
                s = jnp.where(diag * tk + krow <= qcol, s, NEG_INF)
            scores.append(s)
        return _softmax_heads_t(scores, [vt_ref[j, _head(h), :] for h in range(nh)], state)

    state = _init_state_t(tq, nh)
    ndiag = tq // tk
    for d in range(ndiag):
        state = block(i * ndiag + d, state, d)
    state = lax.fori_loop(0, i * ndiag, lambda j, st: block(j, st, None), state)
    for h in range(nh):
        _, l, acc = state[h]
        o_ref[:, _head(h)] = ((acc / l).T * _silu(g_ref[:, _head(h)])).astype(o_ref.dtype)


def fox_attention(p, ccol, crow, batch, seq, *, tq=512, tk=256):
    nq = seq // tq
    return pl.pallas_call(
        functools.partial(_fox_kernel, tq=tq, tk=tk, scale=HEAD_DIM ** -0.5),
        grid=(batch, nq),
        in_specs=[
            pl.BlockSpec((tq, GROUP), lambda b, i: (b * nq + i, C_FOX_Q // GROUP)),
            pl.BlockSpec((seq, GROUP), lambda b, i: (b, C_FOX_K // GROUP)),
            pl.BlockSpec((seq, GROUP), lambda b, i: (b, C_FOX_V // GROUP)),
            pl.BlockSpec((tq, GROUP), lambda b, i: (b * nq + i, C_FOX_G // GROUP)),
            pl.BlockSpec((seq, LANES), lambda b, i: (b, 0)),
            pl.BlockSpec((None, 8, seq), lambda b, i: (b, 0, 0)),
        ],
        out_specs=pl.BlockSpec((tq, GROUP), lambda b, i: (b * nq + i, 0)),
        out_shape=jax.ShapeDtypeStruct((batch * seq, GROUP), BF16),
        scratch_shapes=[pltpu.VMEM((seq, GROUP), BF16), pltpu.VMEM((seq // tk, GROUP, tk), BF16)],
        compiler_params=_params("parallel", "arbitrary"),
        name="fox_attention",
    )(p, p, p, p, ccol, crow)


def _mla_kernel(q_ref, kv_ref, kr_ref, g_ref, cosq_ref, sinq_ref, cos_ref, sin_ref,
                o_ref, kb_ref, vt_ref, *, tq, tk, scale):
    i = pl.program_id(1)
    nh = N_HEADS
    hw = 2 * HEAD_DIM
    seq = kv_ref.shape[0]

    @pl.when(i == 0)
    def _():
        def body(c, _):
            rows = pl.ds(pl.multiple_of(c * tk, tk), tk)
            kr = _rope(kr_ref[rows, :], cos_ref[rows, :], sin_ref[rows, :]).astype(BF16)
            for h in range(nh):
                kb_ref[rows, h * hw:h * hw + HEAD_DIM] = kv_ref[rows, h * hw:h * hw + HEAD_DIM]
                kb_ref[rows, h * hw + HEAD_DIM:(h + 1) * hw] = kr
                v = kv_ref[rows, h * hw + HEAD_DIM:(h + 1) * hw].astype(F32)
                vt_ref[c, _head(h), :] = v.T.astype(BF16)
            return 0
        lax.fori_loop(0, seq // tk, body, 0)

    cosq = cosq_ref[...]
    sinq = sinq_ref[...]
    qts = []
    for h in range(nh):
        qn = q_ref[:, h * hw:h * hw + HEAD_DIM] * scale
        qr = _rope(q_ref[:, h * hw + HEAD_DIM:(h + 1) * hw], cosq, sinq) * scale
        qts.append(jnp.concatenate([qn.T, qr.T], axis=0).astype(BF16))
    krow = lax.broadcasted_iota(jnp.int32, (tk, tq), 0)
    qcol = lax.broadcasted_iota(jnp.int32, (tk, tq), 1)

    def block(j, state, diag):
        off = pl.multiple_of(j * tk, tk)
        scores = [_dot(kb_ref[pl.ds(off, tk), h * hw:(h + 1) * hw], qts[h]) for h in range(nh)]
        if diag is not None:
            scores = [jnp.where(diag * tk + krow <= qcol, s, NEG_INF) for s in scores]
        return _softmax_heads_t(scores, [vt_ref[j, _head(h), :] for h in range(nh)], state)

    state = _init_state_t(tq, nh)
    ndiag = tq // tk
    for d in range(ndiag):
        state = block(i * ndiag + d, state, d)
    state = lax.fori_loop(0, i * ndiag, lambda j, st: block(j, st, None), state)
    for h in range(nh):
        _, l, acc = state[h]
        o_ref[:, _head(h)] = ((acc / l).T * _silu(g_ref[:, _head(h)])).astype(o_ref.dtype)


def mla_attention(p, qp, kvp, cos_m, sin_m, batch, seq, *, tq=512, tk=256):
    nq = seq // tq
    hb = HEAD_DIM
    wide = N_HEADS * 2 * hb
    return pl.pallas_call(
        functools.partial(_mla_kernel, tq=tq, tk=tk, scale=(MLA_NOPE + MLA_ROPE) ** -0.5),
        grid=(batch, nq),
        in_specs=[
            pl.BlockSpec((tq, wide), lambda b, i: (b * nq + i, 0)),
            pl.BlockSpec((seq, wide), lambda b, i: (b, 0)),
            pl.BlockSpec((seq, hb), lambda b, i: (b, C_MLA_KR // hb)),
            pl.BlockSpec((tq, GROUP), lambda b, i: (b * nq + i, C_MLA_G // GROUP)),
            pl.BlockSpec((tq, hb), lambda b, i: (i, 0)),
            pl.BlockSpec((tq, hb), lambda b, i: (i, 0)),
            pl.BlockSpec((seq, hb), lambda b, i: (0, 0)),
            pl.BlockSpec((seq, hb), lambda b, i: (0, 0)),
        ],
        out_specs=pl.BlockSpec((tq, GROUP), lambda b, i: (b * nq + i, 0)),
        out_shape=jax.ShapeDtypeStruct((batch * seq, GROUP), BF16),
        scratch_shapes=[pltpu.VMEM((seq, wide), BF16), pltpu.VMEM((seq // tk, GROUP, tk), BF16)],
        compiler_params=_params("parallel", "arbitrary"),
        name="mla_attention",
    )(qp, kvp, p, p, cos_m, sin_m, cos_m, sin_m)


def _compress_kernel(tk_ref, tv_ref, posk_ref, posv_ref, w1k_ref, w1v_ref, w2k_ref, w2v_ref,
                     cos_ref, sin_ref, kc_ref, vc_ref):
    half = tk_ref.shape[1]

    def mlp(t_ref, pos_ref, w1_ref, w2_ref):
        t = t_ref[...]
        a = _dot((t + pos_ref[0:1, :]).astype(BF16), w1_ref[0:half, :])
        bm = _dot((t + pos_ref[1:2, :]).astype(BF16), w1_ref[half:2 * half, :])
        hidden = a + pltpu.roll(bm, bm.shape[0] - 1, 0)
        return _dot(_silu(hidden).astype(BF16), w2_ref[...])

    kc = mlp(tk_ref, posk_ref, w1k_ref, w2k_ref)
    kc_ref[...] = _rope(kc, cos_ref[...], sin_ref[...]).astype(BF16)
    vc_ref[...] = mlp(tv_ref, posv_ref, w1v_ref, w2v_ref).astype(BF16)


def nsa_compress(t2k, t2v, posk, posv, w1k, w1v, w2k, w2v, cos_c, sin_c, batch):
    ng, width = t2k.shape[1], t2k.shape[2]
    full = lambda a: pl.BlockSpec(a.shape, lambda b: (0,) * a.ndim)
    return pl.pallas_call(
        _compress_kernel,
        grid=(batch,),
        in_specs=[
            pl.BlockSpec((None, ng, width), lambda b: (b, 0, 0)),
            pl.BlockSpec((None, ng, width), lambda b: (b, 0, 0)),
            full(posk), full(posv), full(w1k), full(w1v), full(w2k), full(w2v), full(cos_c), full(sin_c),
        ],
        out_specs=[pl.BlockSpec((None, ng, HEAD_DIM), lambda b: (b, 0, 0)),
                   pl.BlockSpec((None, ng, HEAD_DIM), lambda b: (b, 0, 0))],
        out_shape=[jax.ShapeDtypeStruct((batch, ng, HEAD_DIM), BF16),
                   jax.ShapeDtypeStruct((batch, ng, HEAD_DIM), BF16)],
        compiler_params=_params("parallel"),
        name="nsa_compress",
    )(t2k, t2v, posk, posv, w1k, w1v, w2k, w2v, cos_c, sin_c)


def _nsa_kernel(q_ref, ks_ref, vs_ref, kw_ref, vw_ref, kc_ref, vc_ref, misc_ref, g_ref,
                cosq_ref, sinq_ref, cos_ref, sin_ref, c2s_ref, et_ref, o_ref,
                ksb_ref, vst_ref, kwb_ref, vwt_ref, vct_ref, *, tq, tk, scale, n_sel):
    i = pl.program_id(1)
    seq = ks_ref.shape[0]
    nh = N_HEADS
    ndiag = tq // tk
    n_all = nh * tq

    @pl.when(i == 0)
    def _():
        vct_ref[...] = vc_ref[...].astype(F32).T.astype(BF16)

        def body(c, _):
            rows = pl.ds(pl.multiple_of(c * tk, tk), tk)
            cos = cos_ref[rows, :]
            sin = sin_ref[rows, :]
            ksb_ref[rows, :] = _rope(ks_ref[rows, :], cos, sin).astype(BF16)
            kwb_ref[rows, :] = _rope(kw_ref[rows, :], cos, sin).astype(BF16)
            vst_ref[c] = vs_ref[rows, :].T.astype(BF16)
            vwt_ref[c] = vw_ref[rows, :].T.astype(BF16)
            return 0
        lax.fori_loop(0, seq // tk, body, 0)

    cosq = cosq_ref[...]
    sinq = sinq_ref[...]
    qt = jnp.concatenate([(_rope(q_ref[:, _head(h)], cosq, sinq) * scale).T.astype(BF16)
                          for h in range(nh)], axis=1)

    def heads(x):
        return jnp.concatenate([x] * nh, axis=1)

    n_row = lax.broadcasted_iota(jnp.int32, (LANES, n_all), 0)
    qpos_c = i * tq + (lax.broadcasted_iota(jnp.int32, (LANES, n_all), 1) & (tq - 1))
    zc = _dot(kc_ref[...], qt)
    mask_c = (n_row * CMP_STRIDE + (CMP_LEN - 1)) <= qpos_c
    mc = jnp.max(jnp.where(mask_c, zc, NEG_INF), axis=0, keepdims=True)
    pc = jnp.where(mask_c, jnp.exp(zc - mc), 0.0)
    lc = jnp.sum(pc, axis=0, keepdims=True)
    pc = pc / jnp.where(lc > 0.0, lc, 1.0)
    o_cmp = _dot(vct_ref[...], pc.astype(BF16))

    pc_sum = pc[:, 0:tq]
    for h in range(1, nh):
        pc_sum = pc_sum + pc[:, h * tq:(h + 1) * tq]
    hi, lo = _split2(pc_sum)
    imp = _dot(c2s_ref[...], hi) + _dot(c2s_ref[...], lo)
    blk = lax.broadcasted_iota(jnp.int32, (n_sel, tq), 0)
    cur = (i * tq + lax.broadcasted_iota(jnp.int32, (n_sel, tq), 1)) >> 6
    valid = blk <= cur
    forced = (blk == 0) | (blk == cur) | (blk == cur - 1)
    score = jnp.where(valid, jnp.where(forced, FORCED_BONUS, imp), NEG_INF)
    rank = jnp.zeros((n_sel, tq), F32)
    for k in range(n_sel):
        sk = score[k:k + 1, :]
        later = jnp.where(blk > k, 1.0, 0.0)
        rank = rank + jnp.where(sk > score, 1.0, 0.0) + jnp.where(sk == score, later, 0.0)
    sel_t = jnp.where(rank < float(SEL_TOPN), 1.0, 0.0)
    sel_t = jnp.concatenate([sel_t, jnp.zeros((LANES - n_sel, tq), F32)], axis=0).astype(BF16)

    krow = lax.broadcasted_iota(jnp.int32, (tk, n_all), 0)
    qcol = lax.broadcasted_iota(jnp.int32, (tk, n_all), 1) & (tq - 1)

    state0 = _init_state_t(n_all, 1)[0]

    def sel_block(j, st, diag):
        off = pl.multiple_of(j * tk, tk)
        s = _dot(ksb_ref[pl.ds(off, tk), :], qt)
        chosen = heads(_dot(et_ref[j], sel_t)) > 0.5
        s = jnp.where(chosen, s, NEG_INF)
        if diag is not None:
            s = jnp.where(diag * tk + krow <= qcol, s, NEG_INF)
        return _softmax_step_t(s, vst_ref[j], *st)

    st = lax.fori_loop(0, i * ndiag, lambda j, s: sel_block(j, s, None), state0)
    for d in range(ndiag):
        st = sel_block(i * ndiag + d, st, d)
    o_slc = st[2] / st[1]

    st = state0
    for rel in list(range(ndiag)) + list(range(-1, -WINDOW // tk - 1, -1)):
        jj = i * ndiag + rel
        jc = jnp.maximum(jj, 0)
        s = _dot(kwb_ref[pl.ds(pl.multiple_of(jc * tk, tk), tk), :], qt)
        if rel >= 0:
            s = jnp.where(rel * tk + krow <= qcol, s, NEG_INF)
        else:
            s = jnp.where(jj >= 0, s, NEG_INF)
            s = jnp.where(rel * tk + krow > qcol - WINDOW, s, NEG_INF)
        st = _softmax_step_t(s, vwt_ref[jc], *st)
    o_win = st[2] / st[1]

    gates_t = _sigmoid(misc_ref[...]).T
    for h in range(nh):
        cols = slice(h * tq, (h + 1) * tq)
        c0 = MISC_BRANCH + 3 * h
        o = (gates_t[c0:c0 + 1, :] * o_cmp[:, cols] + gates_t[c0 + 1:c0 + 2, :] * o_slc[:, cols]
             + gates_t[c0 + 2:c0 + 3, :] * o_win[:, cols])
        o_ref[:, _head(h)] = (o.T * _silu(g_ref[:, _head(h)])).astype(o_ref.dtype)


def nsa_attention(p, kc, vc, cos_t, sin_t, batch, seq, *, tq=512, tk=256):
    nq = seq // tq
    nk = seq // tk
    hb = HEAD_DIM
    n_cmp = (seq - CMP_LEN) // CMP_STRIDE + 1
    n_sel = seq // SEL_LEN
    ncp = kc.shape[1]
    assert ncp == LANES and n_cmp <= ncp and n_sel <= LANES and n_sel % 16 == 0
    assert SEL_LEN == 64 and WINDOW % tk == 0 and WINDOW <= tq and tq % tk == 0 and tk % LANES == 0
    assert tq & (tq - 1) == 0
    cmp_start = np.arange(n_cmp) * CMP_STRIDE
    sel_start = np.arange(n_sel) * SEL_LEN
    overlap = np.clip(np.minimum(cmp_start[:, None] + CMP_LEN, sel_start[None, :] + SEL_LEN)
                      - np.maximum(cmp_start[:, None], sel_start[None, :]), 0, None)
    c2s = np.zeros((n_sel, ncp), np.float32)
    c2s[:, :n_cmp] = (overlap / CMP_LEN).T
    expand = np.zeros((nk, tk, LANES), np.float32)
    keys = np.arange(seq)
    expand[keys // tk, keys % tk, keys // SEL_LEN] = 1.0
    kv = lambda c: pl.BlockSpec((seq, hb), lambda b, i: (b, c // hb))
    return pl.pallas_call(
        functools.partial(_nsa_kernel, tq=tq, tk=tk, scale=HEAD_DIM ** -0.5, n_sel=n_sel),
        grid=(batch, nq),
        in_specs=[
            pl.BlockSpec((tq, GROUP), lambda b, i: (b * nq + i, C_NSA_Q // GROUP)),
            kv(C_NSA_KS), kv(C_NSA_VS), kv(C_NSA_KW), kv(C_NSA_VW),
            pl.BlockSpec((None, ncp, hb), lambda b, i: (b, 0, 0)),
            pl.BlockSpec((None, ncp, hb), lambda b, i: (b, 0, 0)),
            pl.BlockSpec((tq, LANES), lambda b, i: (b * nq + i, C_MISC // LANES)),
            pl.BlockSpec((tq, GROUP), lambda b, i: (b * nq + i, C_NSA_G // GROUP)),
            pl.BlockSpec((tq, hb), lambda b, i: (i, 0)),
            pl.BlockSpec((tq, hb), lambda b, i: (i, 0)),
            pl.BlockSpec((seq, hb), lambda b, i: (0, 0)),
            pl.BlockSpec((seq, hb), lambda b, i: (0, 0)),
            pl.BlockSpec((n_sel, ncp), lambda b, i: (0, 0)),
            pl.BlockSpec((nk, tk, LANES), lambda b, i: (0, 0, 0)),
        ],
        out_specs=pl.BlockSpec((tq, GROUP), lambda b, i: (b * nq + i, 0)),
        out_shape=jax.ShapeDtypeStruct((batch * seq, GROUP), BF16),
        scratch_shapes=[pltpu.VMEM((seq, hb), BF16), pltpu.VMEM((nk, hb, tk), BF16),
                        pltpu.VMEM((seq, hb), BF16), pltpu.VMEM((nk, hb, tk), BF16),
                        pltpu.VMEM((hb, ncp), BF16)],
        compiler_params=_params("parallel", "arbitrary"),
        name="nsa_attention",
    )(p, p, p, p, p, kc, vc, p, p, cos_t, sin_t, cos_t, sin_t,
      jnp.asarray(c2s, BF16), jnp.asarray(expand, BF16))


def _out_kernel(ma_ref, mb_ref, mc_ref, md_ref, w_ref, g_ref, x_ref, o_ref):
    y = _dot(ma_ref[...], w_ref[0])
    y = y + _dot(mb_ref[...], w_ref[1])
    y = y + _dot(mc_ref[...], w_ref[2])
    y = y + _dot(md_ref[...], w_ref[3])
    ms = jnp.mean(y * y, axis=-1, keepdims=True)
    o_ref[...] = x_ref[...] + y * lax.rsqrt(ms + RMS_EPS) * g_ref[...]


def out_projection(mixes, w, g, x, *, tm=256):
    m, d = x.shape
    mix_spec = pl.BlockSpec((tm, GROUP), lambda i: (i, 0))
    return pl.pallas_call(
        _out_kernel,
        grid=(m // tm,),
        in_specs=[mix_spec, mix_spec, mix_spec, mix_spec,
                  pl.BlockSpec((len(mixes), GROUP, d), lambda i: (0, 0, 0)),
                  pl.BlockSpec((1, d), lambda i: (0, 0)),
                  pl.BlockSpec((tm, d), lambda i: (i, 0))],
        out_specs=pl.BlockSpec((tm, d), lambda i: (i, 0)),
        out_shape=jax.ShapeDtypeStruct((m, d), F32),
        compiler_params=_params("parallel"),
        name="out_projection",
    )(*mixes, w, g, x)


def _spread_rope(w):
    z = jnp.zeros(w.shape[:-1] + (MLA_ROPE // 2,), w.dtype)
    return jnp.concatenate([w[..., :MLA_ROPE // 2], z, w[..., MLA_ROPE // 2:], z], axis=-1)


def _pack_in_projection(w):
    def cols(name):
        o, n = _ORIG[name]
        return w[..., o:o + n]

    def zeros(n):
        return jnp.zeros(w.shape[:-1] + (n,), w.dtype)

    misc = jnp.concatenate([cols("fox_f"), zeros(MISC_BRANCH - 4), cols("nsa_branch"),
                            zeros(LANES - MISC_BRANCH - 12)], axis=-1)
    packed = jnp.concatenate([
        cols("sb_q"), cols("sb_k"), cols("sb_v"), cols("sb_gate"),
        cols("nsa_q"), cols("nsa_k_cmp"), cols("nsa_v_cmp"), cols("nsa_k_sel"), cols("nsa_v_sel"),
        cols("nsa_k_win"), cols("nsa_v_win"), misc, cols("mla_ckv"), cols("nsa_gate"),
        cols("fox_q"), cols("fox_k"), cols("fox_v"), cols("fox_gate"),
        cols("mla_cq"), _spread_rope(cols("mla_k_rope")), cols("mla_gate"),
    ], axis=-1)
    assert packed.shape[-1] == P_WIDTH
    return packed


def _pack_uq(w_uq):
    r = w_uq.shape[0]
    w = w_uq.reshape(r, N_HEADS, MLA_NOPE + MLA_ROPE)
    w = jnp.concatenate([w[..., :MLA_NOPE], _spread_rope(w[..., MLA_NOPE:])], axis=-1)
    return w.reshape(r, N_HEADS * 2 * HEAD_DIM)


def _rope_tables(pos, half, spread):
    inv_freq = ROPE_THETA ** (-jnp.arange(half, dtype=F32) / half)
    ang = pos.astype(F32)[:, None] * inv_freq[None, :]
    cos, sin = jnp.cos(ang), jnp.sin(ang)
    if spread:
        z = jnp.zeros_like(cos)
        return jnp.concatenate([cos, z, cos, z], axis=1), jnp.concatenate([-sin, z, sin, z], axis=1)
    return jnp.concatenate([cos, cos], axis=1), jnp.concatenate([-sin, sin], axis=1)


def _layer(x2, batch, seq, pre_g, post_g, w_in, b_in, w_out, forget_bias,
           pos_k, w1_k, w2_k, pos_v, w1_v, w2_v, q_norm_g, w_uq, kv_norm_g, w_ukv, tables):
    d_model = x2.shape[1]
    cos_t, sin_t, cos_c, sin_c, cos_m, sin_m = tables

    w_p = _pack_in_projection(w_in.astype(BF16))
    b_p = _pack_in_projection(b_in[None, :])
    p = norm_matmul(x2, 0, d_model, pre_g[None, :], w_p, b_p, tm=1024, tn=512, out_dtype=F32)

    o_sb = sb_attention(p, batch, seq)

    ng = seq // CMP_STRIDE
    width = CMP_STRIDE * HEAD_DIM
    t2k = p[:, C_NSA_KC:C_NSA_KC + HEAD_DIM].reshape(batch, ng, width)
    t2v = p[:, C_NSA_VC:C_NSA_VC + HEAD_DIM].reshape(batch, ng, width)
    kc, vc = nsa_compress(t2k, t2v, pos_k.reshape(2, width), pos_v.reshape(2, width),
                          w1_k.astype(BF16), w1_v.astype(BF16), w2_k.astype(BF16), w2_v.astype(BF16),
                          cos_c, sin_c, batch)
    o_nsa = nsa_attention(p, kc, vc, cos_t, sin_t, batch, seq)

    bias_row = jnp.zeros((1, LANES), F32).at[0, MISC_FOX:MISC_FOX + N_HEADS].set(forget_bias)
    ccol, crow = fox_prep(p, bias_row, batch, seq)
    o_fox = fox_attention(p, ccol, crow, batch, seq)

    qp = norm_matmul(p, C_MLA_CQ // MLA_Q_RANK, MLA_Q_RANK, q_norm_g[None, :], _pack_uq(w_uq).astype(BF16),
                     jnp.zeros((1, N_HEADS * 2 * HEAD_DIM), F32), tm=1024, tn=1024, out_dtype=F32)
    kvp = norm_matmul(p, C_MLA_CKV // MLA_KV_RANK, MLA_KV_RANK, kv_norm_g[None, :], w_ukv.astype(BF16),
                      jnp.zeros((1, N_HEADS * 2 * HEAD_DIM), F32), tm=1024, tn=1024, out_dtype=BF16)
    o_mla = mla_attention(p, qp, kvp, cos_m, sin_m, batch, seq)

    w_o = w_out.astype(BF16).reshape(4, GROUP, d_model)
    return out_projection((o_sb, o_nsa, o_fox, o_mla), w_o, post_g[None, :], x2)


def kernel(x, pre_norm_g, post_norm_g, w_in, b_in, w_out, fox_forget_bias, nsa_cmp_pos_k, nsa_cmp_w1_k, nsa_cmp_w2_k, nsa_cmp_pos_v, nsa_cmp_w1_v, nsa_cmp_w2_v, mla_q_norm_g, mla_w_uq, mla_kv_norm_g, mla_w_ukv):
    batch, seq, d_model = x.shape
    depth = w_in.shape[0]
    pos = jnp.arange(seq)
    cmp_end = jnp.arange(seq // CMP_STRIDE) * CMP_STRIDE + (CMP_LEN - 1)
    tables = (_rope_tables(pos, HEAD_DIM // 2, False) + _rope_tables(cmp_end, HEAD_DIM // 2, False)
              + _rope_tables(pos, MLA_ROPE // 2, True))
    x2 = x.reshape(batch * seq, d_model)
    for l in range(depth):
        x2 = _layer(x2, batch, seq, pre_norm_g[l], post_norm_g[l], w_in[l], b_in[l], w_out[l],
                    fox_forget_bias[l], nsa_cmp_pos_k[l], nsa_cmp_w1_k[l], nsa_cmp_w2_k[l],
                    nsa_cmp_pos_v[l], nsa_cmp_w1_v[l], nsa_cmp_w2_v[l],
                    mla_q_norm_g[l], mla_w_uq[l], mla_kv_norm_g[l], mla_w_ukv[l], tables)
    return x2.reshape(batch, seq, d_model)
```

```python
import functools

import numpy as np
import jax
import jax.numpy as jnp
from jax import lax
from jax.experimental import pallas as pl
from jax.experimental.pallas import tpu as pltpu

F32 = jnp.float32
BF16 = jnp.bfloat16

LANES = 128
HEAD_DIM = 128
N_HEADS = 4
GROUP = N_HEADS * HEAD_DIM
RMS_EPS = 1e-6
NEG_INF = -1e30
ROPE_THETA = 10000.0

CMP_LEN = 32
CMP_STRIDE = 16
SEL_LEN = 64
SEL_TOPN = 16
WINDOW = 512
FORCED_BONUS = 1e6

MLA_Q_RANK = 384
MLA_KV_RANK = 128
MLA_NOPE = 128
MLA_ROPE = 64

VMEM_LIMIT = 56 * 1024 * 1024

C_SB_Q, C_SB_K, C_SB_V, C_SB_G = 0, 512, 1024, 1536
C_NSA_Q = 2048
C_NSA_KC, C_NSA_VC, C_NSA_KS, C_NSA_VS, C_NSA_KW, C_NSA_VW = 2560, 2688, 2816, 2944, 3072, 3200
C_MISC = 3328
C_MLA_CKV = 3456
C_NSA_G = 3584
C_FOX_Q, C_FOX_K, C_FOX_V, C_FOX_G = 4096, 4608, 5120, 5632
C_MLA_CQ = 6144
C_MLA_KR = 6528
C_MLA_G = 6656
P_WIDTH = 7168
assert all(c % GROUP == 0 for c in (C_SB_Q, C_SB_K, C_SB_V, C_SB_G, C_NSA_Q, C_NSA_G, C_FOX_Q, C_FOX_K,
                                    C_FOX_V, C_FOX_G, C_MLA_G)) and C_MLA_CQ % MLA_Q_RANK == 0
MISC_FOX = 0
MISC_BRANCH = 8

_ORIG = {}
_off = 0
for _name, _w in (("sb_q", 512), ("sb_k", 512), ("sb_v", 512), ("sb_gate", 512),
                  ("nsa_q", 512), ("nsa_k_cmp", 128), ("nsa_v_cmp", 128), ("nsa_k_sel", 128),
                  ("nsa_v_sel", 128), ("nsa_k_win", 128), ("nsa_v_win", 128), ("nsa_branch", 12),
                  ("nsa_gate", 512), ("fox_q", 512), ("fox_k", 512), ("fox_v", 512), ("fox_f", 4),
                  ("fox_gate", 512), ("mla_cq", 384), ("mla_ckv", 128), ("mla_k_rope", 64),
                  ("mla_gate", 512)):
    _ORIG[_name] = (_off, _w)
    _off += _w
IN_WIDTH = _off


def _params(*sem):
    return pltpu.CompilerParams(dimension_semantics=sem, vmem_limit_bytes=VMEM_LIMIT)


def _dot(a, b):
    return jnp.dot(a, b, preferred_element_type=F32)


def _split2(x):
    hi = x.astype(BF16)
    lo = (x - hi.astype(F32)).astype(BF16)
    return hi, lo


def _split3(x):
    hi = x.astype(BF16)
    r = x - hi.astype(F32)
    mid = r.astype(BF16)
    lo = (r - mid.astype(F32)).astype(BF16)
    return hi, mid, lo


def _rope(x, cos, sin_signed):
    return x * cos + pltpu.roll(x, 64, 1) * sin_signed


def _silu(g):
    return g / (1.0 + jnp.exp(-g))


def _sigmoid(g):
    return 1.0 / (1.0 + jnp.exp(-g))


def _head(h):
    return slice(h * HEAD_DIM, (h + 1) * HEAD_DIM)


def _cast_rows(dst_ref, src_ref, chunk):
    def body(c, _):
        rows = pl.ds(pl.multiple_of(c * chunk, chunk), chunk)
        dst_ref[rows, :] = src_ref[rows, :].astype(dst_ref.dtype)
        return 0
    lax.fori_loop(0, src_ref.shape[0] // chunk, body, 0)


def _norm_matmul_kernel(x_ref, g_ref, w_ref, b_ref, o_ref, h_ref, *, chunk):
    tm = x_ref.shape[0]

    @pl.when(pl.program_id(1) == 0)
    def _():
        def body(c, _):
            rows = pl.ds(pl.multiple_of(c * chunk, chunk), chunk)
            x = x_ref[rows, :]
            ms = jnp.mean(x * x, axis=-1, keepdims=True)
            h_ref[rows, :] = (x * lax.rsqrt(ms + RMS_EPS) * g_ref[...]).astype(BF16)
            return 0
        lax.fori_loop(0, tm // chunk, body, 0)

    o_ref[...] = (_dot(h_ref[...], w_ref[...]) + b_ref[...]).astype(o_ref.dtype)


def norm_matmul(x, xcol, kdim, g, w, b, *, tm, tn, out_dtype):
    m = x.shape[0]
    n = w.shape[1]
    chunk = min(tm, 256)
    return pl.pallas_call(
        functools.partial(_norm_matmul_kernel, chunk=chunk),
        grid=(m // tm, n // tn),
        in_specs=[
            pl.BlockSpec((tm, kdim), lambda i, j: (i, xcol)),
            pl.BlockSpec((1, kdim), lambda i, j: (0, 0)),
            pl.BlockSpec((kdim, tn), lambda i, j: (0, j)),
            pl.BlockSpec((1, tn), lambda i, j: (0, j)),
        ],
        out_specs=pl.BlockSpec((tm, tn), lambda i, j: (i, j)),
        out_shape=jax.ShapeDtypeStruct((m, n), out_dtype),
        scratch_shapes=[pltpu.VMEM((tm, kdim), BF16)],
        compiler_params=_params("parallel", "arbitrary"),
        name="norm_matmul",
    )(x, g, w, b)


def _softmax_probs_t(s, m, l):
    m_new = jnp.maximum(m, jnp.max(s, axis=0, keepdims=True))
    alpha = jnp.exp(m - m_new)
    p = jnp.exp(s - m_new)
    l = alpha * l + jnp.sum(p, axis=0, keepdims=True)
    return m_new, l, alpha, p.astype(BF16)


def _softmax_step_t(s, vt, m, l, acc):
    m, l, alpha, p = _softmax_probs_t(s, m, l)
    return m, l, alpha * acc + _dot(vt, p)


def _softmax_heads_t(scores, vts, state):
    probs = [_softmax_probs_t(s, st[0], st[1]) for s, st in zip(scores, state)]
    return tuple((m, l, alpha * st[2] + _dot(vt, p)) for (m, l, alpha, p), vt, st in zip(probs, vts, state))


def _init_state_t(n, nh):
    return tuple((jnp.full((1, n), NEG_INF, F32), jnp.zeros((1, n), F32), jnp.zeros((HEAD_DIM, n), F32))
                 for _ in range(nh))


def _transpose_blocks(vt_ref, v_ref, tk):
    def body(c, _):
        x = v_ref[pl.ds(pl.multiple_of(c * tk, tk), tk), :].astype(F32)
        for h in range(x.shape[1] // HEAD_DIM):
            vt_ref[c, _head(h), :] = x[:, _head(h)].T.astype(vt_ref.dtype)
        return 0
    lax.fori_loop(0, v_ref.shape[0] // tk, body, 0)


def _sb_kernel(q_ref, k_ref, v_ref, g_ref, ut_ref, o_ref, kb_ref, vt_ref, *, tq, gk, tk, scale):
    i = pl.program_id(1)
    nh = N_HEADS
    nsub = gk // tk

    @pl.when(i == 0)
    def _():
        _cast_rows(kb_ref, k_ref, 256)
        _transpose_blocks(vt_ref, v_ref, gk)

    ut = ut_ref[...]
    qts = [(q_ref[:, _head(h)] * scale).T.astype(BF16) for h in range(nh)]
    krow = lax.broadcasted_iota(jnp.int32, (tk, tq), 0)
    qcol = lax.broadcasted_iota(jnp.int32, (tk, tq), 1)

    def group(gj, carries, accs, diag):
        masked = diag is not None
        goff = pl.multiple_of(gj * gk, gk)
        pairs = [(h, c) for h in range(nh) for c in reversed(range(nsub))]
        zs = {hc: _dot(kb_ref[pl.ds(goff + hc[1] * tk, tk), _head(hc[0])], qts[hc[0]]) for hc in pairs}
        log_beta, log_keep, log_after, masks = {}, {}, {}, {}
        for hc in pairs:
            z = zs[hc]
            softplus = jnp.maximum(z, 0.0) + jnp.log(1.0 + jnp.exp(-jnp.abs(z)))
            lk = -softplus
            log_beta[hc] = z - softplus
            if masked:
                masks[hc] = (diag * gk + hc[1] * tk + krow) < qcol
                lk = jnp.where(masks[hc], lk, 0.0)
            log_keep[hc] = lk
            hi, lo = _split2(lk)
            r = _dot(ut, jnp.concatenate([hi, lo], axis=1))
            log_after[hc] = r[:, :tq] + r[:, tq:]
        new_carries, ws = [], []
        for h in range(nh):
            carry = carries[h]
            w_h = [None] * nsub
            for c in reversed(range(nsub)):
                w = jnp.exp(log_beta[h, c] + log_after[h, c] + carry)
                if masked:
                    w = jnp.where(masks[h, c], w, 0.0)
                w_h[c] = w.astype(BF16)
                carry = carry + jnp.sum(log_keep[h, c], axis=0, keepdims=True)
            ws.append(jnp.concatenate(w_h, axis=0))
            new_carries.append(carry)
        new_accs = [accs[h] + _dot(vt_ref[gj, _head(h), :], ws[h]) for h in range(nh)]
        return tuple(new_carries), tuple(new_accs)

    carries = tuple(jnp.zeros((1, tq), F32) for _ in range(nh))
    accs = tuple(jnp.zeros((HEAD_DIM, tq), F32) for _ in range(nh))
    ndiag = tq // gk
    for d in reversed(range(ndiag)):
        carries, accs = group(i * ndiag + d, carries, accs, d)
    carries, accs = lax.fori_loop(0, i * ndiag, lambda t, c: group(i * ndiag - 1 - t, c[0], c[1], None),
                                  (carries, accs))
    for h in range(nh):
        o_ref[:, _head(h)] = (accs[h].T * _silu(g_ref[:, _head(h)])).astype(o_ref.dtype)


def sb_attention(p, batch, seq, *, tq=512, gk=256, tk=128):
    nq = seq // tq
    ut = jnp.asarray(np.triu(np.ones((tk, tk), np.float32), 1), BF16)
    return pl.pallas_call(
        functools.partial(_sb_kernel, tq=tq, gk=gk, tk=tk, scale=HEAD_DIM ** -0.5),
        grid=(batch, nq),
        in_specs=[
            pl.BlockSpec((tq, GROUP), lambda b, i: (b * nq + i, C_SB_Q // GROUP)),
            pl.BlockSpec((seq, GROUP), lambda b, i: (b, C_SB_K // GROUP)),
            pl.BlockSpec((seq, GROUP), lambda b, i: (b, C_SB_V // GROUP)),
            pl.BlockSpec((tq, GROUP), lambda b, i: (b * nq + i, C_SB_G // GROUP)),
            pl.BlockSpec((tk, tk), lambda b, i: (0, 0)),
        ],
        out_specs=pl.BlockSpec((tq, GROUP), lambda b, i: (b * nq + i, 0)),
        out_shape=jax.ShapeDtypeStruct((batch * seq, GROUP), BF16),
        scratch_shapes=[pltpu.VMEM((seq, GROUP), BF16), pltpu.VMEM((seq // gk, GROUP, gk), BF16)],
        compiler_params=_params("parallel", "arbitrary"),
        name="sb_attention",
    )(p, p, p, p, ut)


def _fox_prep_kernel(f_ref, bias_ref, col_ref, row_ref, *, chunk):
    seq = f_ref.shape[0]
    r = lax.broadcasted_iota(jnp.int32, (chunk, chunk), 0)
    c = lax.broadcasted_iota(jnp.int32, (chunk, chunk), 1)
    tri = jnp.where(c <= r, 1.0, 0.0).astype(BF16)
    carry = jnp.zeros((1, LANES), F32)
    for n in range(seq // chunk):
        x = f_ref[n * chunk:(n + 1) * chunk, :] + bias_ref[...]
        logf = jnp.minimum(x, 0.0) - jnp.log(1.0 + jnp.exp(-jnp.abs(x)))
        hi, mid, lo = _split3(logf)
        within = _dot(tri, hi) + _dot(tri, mid) + _dot(tri, lo)
        cum = within + carry
        col_ref[n * chunk:(n + 1) * chunk, :] = cum
        row_ref[:, n * chunk:(n + 1) * chunk] = cum.T[0:8, :]
        carry = cum[chunk - 1:chunk, :]


def fox_prep(p, bias_row, batch, seq):
    chunk = LANES
    return pl.pallas_call(
        functools.partial(_fox_prep_kernel, chunk=chunk),
        grid=(batch,),
        in_specs=[
            pl.BlockSpec((seq, LANES), lambda b: (b, C_MISC // LANES)),
            pl.BlockSpec((1, LANES), lambda b: (0, 0)),
        ],
        out_specs=[
            pl.BlockSpec((seq, LANES), lambda b: (b, 0)),
            pl.BlockSpec((None, 8, seq), lambda b: (b, 0, 0)),
        ],
        out_shape=[jax.ShapeDtypeStruct((batch * seq, LANES), F32),
                   jax.ShapeDtypeStruct((batch, 8, seq), F32)],
        compiler_params=_params("parallel"),
        name="fox_prep",
    )(p, bias_row)


def _fox_kernel(q_ref, k_ref, v_ref, g_ref, ccol_ref, crow_ref, o_ref, kb_ref, vt_ref, *, tq, tk, scale):
    i = pl.program_id(1)
    nh = N_HEADS

    @pl.when(i == 0)
    def _():
        _cast_rows(kb_ref, k_ref, 256)
        _transpose_blocks(vt_ref, v_ref, tk)

    qts = [(q_ref[:, _head(h)] * scale).T.astype(BF16) for h in range(nh)]
    qoff = pl.multiple_of(i * tq, tq)
    cqs = [crow_ref[h:h + 1, pl.ds(qoff, tq)] for h in range(nh)]
    krow = lax.broadcasted_iota(jnp.int32, (tk, tq), 0)
    qcol = lax.broadcasted_iota(jnp.int32, (tk, tq), 1)

    def block(j, state, diag):
        off = pl.multiple_of(j * tk, tk)
        qk = [_dot(kb_ref[pl.ds(off, tk), _head(h)], qts[h]) for h in range(nh)]
        scores = []
        for h in range(nh):
            ck = ccol_ref[pl.ds(off, tk), h:h + 1]
            s = qk[h] + cqs[h] - ck
            if diag is not None:
                s = jnp.where(diag * tk + krow <= qcol, s, NEG_INF)
            scores.append(s)
        return _softmax_heads_t(scores, [vt_ref[j, _head(h), :] for h in range(nh)], state)

    state = _init_state_t(tq, nh)
    ndiag = tq // tk
    for d in range(ndiag):
        state = block(i * ndiag + d, state, d)
    state = lax.fori_loop(0, i * ndiag, lambda j, st: block(j, st, None), state)
    for h in range(nh):
        _, l, acc = state[h]
        o_ref[:, _head(h)] = ((acc / l).T * _silu(g_ref[:, _head(h)])).astype(o_ref.dtype)


def fox_attention(p, ccol, crow, batch, seq, *, tq=512, tk=256):
    nq = seq // tq
    return pl.pallas_call(
        functools.partial(_fox_kernel, tq=tq, tk=tk, scale=HEAD_DIM ** -0.5),
        grid=(batch, nq),
        in_specs=[
            pl.BlockSpec((tq, GROUP), lambda b, i: (b * nq + i, C_FOX_Q // GROUP)),
            pl.BlockSpec((seq, GROUP), lambda b, i: (b, C_FOX_K // GROUP)),
            pl.BlockSpec((seq, GROUP), lambda b, i: (b, C_FOX_V // GROUP)),
            pl.BlockSpec((tq, GROUP), lambda b, i: (b * nq + i, C_FOX_G // GROUP)),
            pl.BlockSpec((seq, LANES), lambda b, i: (b, 0)),
            pl.BlockSpec((None, 8, seq), lambda b, i: (b, 0, 0)),
        ],
        out_specs=pl.BlockSpec((tq, GROUP), lambda b, i: (b * nq + i, 0)),
        out_shape=jax.ShapeDtypeStruct((batch * seq, GROUP), BF16),
        scratch_shapes=[pltpu.VMEM((seq, GROUP), BF16), pltpu.VMEM((seq // tk, GROUP, tk), BF16)],
        compiler_params=_params("parallel", "arbitrary"),
        name="fox_attention",
    )(p, p, p, p, ccol, crow)


def _mla_kernel(q_ref, kv_ref, kr_ref, g_ref, cosq_ref, sinq_ref, cos_ref, sin_ref,
                o_ref, kb_ref, vt_ref, *, tq, tk, scale):
    i = pl.program_id(1)
    nh = N_HEADS
    hw = 2 * HEAD_DIM
    seq = kv_ref.shape[0]

    @pl.when(i == 0)
    def _():
        def body(c, _):
            rows = pl.ds(pl.multiple_of(c * tk, tk), tk)
            kr = _rope(kr_ref[rows, :], cos_ref[rows, :], sin_ref[rows, :]).astype(BF16)
            for h in range(nh):
                kb_ref[rows, h * hw:h * hw + HEAD_DIM] = kv_ref[rows, h * hw:h * hw + HEAD_DIM]
                kb_ref[rows, h * hw + HEAD_DIM:(h + 1) * hw] = kr
                v = kv_ref[rows, h * hw + HEAD_DIM:(h + 1) * hw].astype(F32)
                vt_ref[c, _head(h), :] = v.T.astype(BF16)
            return 0
        lax.fori_loop(0, seq // tk, body, 0)

    cosq = cosq_ref[...]
    sinq = sinq_ref[...]
    qts = []
    for h in range(nh):
        qn = q_ref[:, h * hw:h * hw + HEAD_DIM] * scale
        qr = _rope(q_ref[:, h * hw + HEAD_DIM:(h + 1) * hw], cosq, sinq) * scale
        qts.append(jnp.concatenate([qn.T, qr.T], axis=0).astype(BF16))
    krow = lax.broadcasted_iota(jnp.int32, (tk, tq), 0)
    qcol = lax.broadcasted_iota(jnp.int32, (tk, tq), 1)

    def block(j, state, diag):
        off = pl.multiple_of(j * tk, tk)
        scores = [_dot(kb_ref[pl.ds(off, tk), h * hw:(h + 1) * hw], qts[h]) for h in range(nh)]
        if diag is not None:
            scores = [jnp.where(diag * tk + krow <= qcol, s, NEG_INF) for s in scores]
        return _softmax_heads_t(scores, [vt_ref[j, _head(h), :] for h in range(nh)], state)

    state = _init_state_t(tq, nh)
    ndiag = tq // tk
    for d in range(ndiag):
        state = block(i * ndiag + d, state, d)
    state = lax.fori_loop(0, i * ndiag, lambda j, st: block(j, st, None), state)
    for h in range(nh):
        _, l, acc = state[h]
        o_ref[:, _head(h)] = ((acc / l).T * _silu(g_ref[:, _head(h)])).astype(o_ref.dtype)


def mla_attention(p, qp, kvp, cos_m, sin_m, batch, seq, *, tq=512, tk=256):
    nq = seq // tq
    hb = HEAD_DIM
    wide = N_HEADS * 2 * hb
    return pl.pallas_call(
        functools.partial(_mla_kernel, tq=tq, tk=tk, scale=(MLA_NOPE + MLA_ROPE) ** -0.5),
        grid=(batch, nq),
        in_specs=[
            pl.BlockSpec((tq, wide), lambda b, i: (b * nq + i, 0)),
            pl.BlockSpec((seq, wide), lambda b, i: (b, 0)),
            pl.BlockSpec((seq, hb), lambda b, i: (b, C_MLA_KR // hb)),
            pl.BlockSpec((tq, GROUP), lambda b, i: (b * nq + i, C_MLA_G // GROUP)),
            pl.BlockSpec((tq, hb), lambda b, i: (i, 0)),
            pl.BlockSpec((tq, hb), lambda b, i: (i, 0)),
            pl.BlockSpec((seq, hb), lambda b, i: (0, 0)),
            pl.BlockSpec((seq, hb), lambda b, i: (0, 0)),
        ],
        out_specs=pl.BlockSpec((tq, GROUP), lambda b, i: (b * nq + i, 0)),
        out_shape=jax.ShapeDtypeStruct((batch * seq, GROUP), BF16),
        scratch_shapes=[pltpu.VMEM((seq, wide), BF16), pltpu.VMEM((seq // tk, GROUP, tk), BF16)],
        compiler_params=_params("parallel", "arbitrary"),
        name="mla_attention",
    )(qp, kvp, p, p, cos_m, sin_m, cos_m, sin_m)


def _compress_kernel(tk_ref, tv_ref, posk_ref, posv_ref, w1k_ref, w1v_ref, w2k_ref, w2v_ref,
                     cos_ref, sin_ref, kc_ref, vc_ref):
    half = tk_ref.shape[1]

    def mlp(t_ref, pos_ref, w1_ref, w2_ref):
        t = t_ref[...]
        a = _dot((t + pos_ref[0:1, :]).astype(BF16), w1_ref[0:half, :])
        bm = _dot((t + pos_ref[1:2, :]).astype(BF16), w1_ref[half:2 * half, :])
        hidden = a + pltpu.roll(bm, bm.shape[0] - 1, 0)
        return _dot(_silu(hidden).astype(BF16), w2_ref[...])

    kc = mlp(tk_ref, posk_ref, w1k_ref, w2k_ref)
    kc_ref[...] = _rope(kc, cos_ref[...], sin_ref[...]).astype(BF16)
    vc_ref[...] = mlp(tv_ref, posv_ref, w1v_ref, w2v_ref).astype(BF16)


def nsa_compress(t2k, t2v, posk, posv, w1k, w1v, w2k, w2v, cos_c, sin_c, batch):
    ng, width = t2k.shape[1], t2k.shape[2]
    full = lambda a: pl.BlockSpec(a.shape, lambda b: (0,) * a.ndim)
    return pl.pallas_call(
        _compress_kernel,
        grid=(batch,),
        in_specs=[
            pl.BlockSpec((None, ng, width), lambda b: (b, 0, 0)),
            pl.BlockSpec((None, ng, width), lambda b: (b, 0, 0)),
            full(posk), full(posv), full(w1k), full(w1v), full(w2k), full(w2v), full(cos_c), full(sin_c),
        ],
        out_specs=[pl.BlockSpec((None, ng, HEAD_DIM), lambda b: (b, 0, 0)),
                   pl.BlockSpec((None, ng, HEAD_DIM), lambda b: (b, 0, 0))],
        out_shape=[jax.ShapeDtypeStruct((batch, ng, HEAD_DIM), BF16),
                   jax.ShapeDtypeStruct((batch, ng, HEAD_DIM), BF16)],
        compiler_params=_params("parallel"),
        name="nsa_compress",
    )(t2k, t2v, posk, posv, w1k, w1v, w2k, w2v, cos_c, sin_c)


def _nsa_kernel(q_ref, ks_ref, vs_ref, kw_ref, vw_ref, kc_ref, vc_ref, misc_ref, g_ref,
                cosq_ref, sinq_ref, cos_ref, sin_ref, c2s_ref, et_ref, o_ref,
                ksb_ref, vst_ref, kwb_ref, vwt_ref, vct_ref, *, tq, tk, scale, n_sel):
    i = pl.program_id(1)
    seq = ks_ref.shape[0]
    nh = N_HEADS
    ndiag = tq // tk
    n_all = nh * tq

    @pl.when(i == 0)
    def _():
        vct_ref[...] = vc_ref[...].astype(F32).T.astype(BF16)

        def body(c, _):
            rows = pl.ds(pl.multiple_of(c * tk, tk), tk)
            cos = cos_ref[rows, :]
            sin = sin_ref[rows, :]
            ksb_ref[rows, :] = _rope(ks_ref[rows, :], cos, sin).astype(BF16)
            kwb_ref[rows, :] = _rope(kw_ref[rows, :], cos, sin).astype(BF16)
            vst_ref[c] = vs_ref[rows, :].T.astype(BF16)
            vwt_ref[c] = vw_ref[rows, :].T.astype(BF16)
            return 0
        lax.fori_loop(0, seq // tk, body, 0)

    cosq = cosq_ref[...]
    sinq = sinq_ref[...]
    qt = jnp.concatenate([(_rope(q_ref[:, _head(h)], cosq, sinq) * scale).T.astype(BF16)
                          for h in range(nh)], axis=1)

    def heads(x):
        return jnp.concatenate([x] * nh, axis=1)

    n_row = lax.broadcasted_iota(jnp.int32, (LANES, n_all), 0)
    qpos_c = i * tq + (lax.broadcasted_iota(jnp.int32, (LANES, n_all), 1) & (tq - 1))
    zc = _dot(kc_ref[...], qt)
    mask_c = (n_row * CMP_STRIDE + (CMP_LEN - 1)) <= qpos_c
    mc = jnp.max(jnp.where(mask_c, zc, NEG_INF), axis=0, keepdims=True)
    pc = jnp.where(mask_c, jnp.exp(zc - mc), 0.0)
    lc = jnp.sum(pc, axis=0, keepdims=True)
    pc = pc / jnp.where(lc > 0.0, lc, 1.0)
    o_cmp = _dot(vct_ref[...], pc.astype(BF16))

    pc_sum = pc[:, 0:tq]
    for h in range(1, nh):
        pc_sum = pc_sum + pc[:, h * tq:(h + 1) * tq]
    hi, lo = _split2(pc_sum)
    imp = _dot(c2s_ref[...], hi) + _dot(c2s_ref[...], lo)
    blk = lax.broadcasted_iota(jnp.int32, (n_sel, tq), 0)
    cur = (i * tq + lax.broadcasted_iota(jnp.int32, (n_sel, tq), 1)) >> 6
    valid = blk <= cur
    forced = (blk == 0) | (blk == cur) | (blk == cur - 1)
    score = jnp.where(valid, jnp.where(forced, FORCED_BONUS, imp), NEG_INF)
    rank = jnp.zeros((n_sel, tq), F32)
    for k in range(n_sel):
        sk = score[k:k + 1, :]
        later = jnp.where(blk > k, 1.0, 0.0)
        rank = rank + jnp.where(sk > score, 1.0, 0.0) + jnp.where(sk == score, later, 0.0)
    sel_t = jnp.where(rank < float(SEL_TOPN), 1.0, 0.0)
    sel_t = jnp.concatenate([sel_t, jnp.zeros((LANES - n_sel, tq), F32)], axis=0).astype(BF16)

    krow = lax.broadcasted_iota(jnp.int32, (tk, n_all), 0)
    qcol = lax.broadcasted_iota(jnp.int32, (tk, n_all), 1) & (tq - 1)

    state0 = _init_state_t(n_all, 1)[0]

    def sel_block(j, st, diag):
        off = pl.multiple_of(j * tk, tk)
        s = _dot(ksb_ref[pl.ds(off, tk), :], qt)
        chosen = heads(_dot(et_ref[j], sel_t)) > 0.5
        s = jnp.where(chosen, s, NEG_INF)
        if diag is not None:
            s = jnp.where(diag * tk + krow <= qcol, s, NEG_INF)
        return _softmax_step_t(s, vst_ref[j], *st)

    st = lax.fori_loop(0, i * ndiag, lambda j, s: sel_block(j, s, None), state0)
    for d in range(ndiag):
        st = sel_block(i * ndiag + d, st, d)
    o_slc = st[2] / st[1]

    st = state0
    for rel in list(range(ndiag)) + list(range(-1, -WINDOW // tk - 1, -1)):
        jj = i * ndiag + rel
        jc = jnp.maximum(jj, 0)
        s = _dot(kwb_ref[pl.ds(pl.multiple_of(jc * tk, tk), tk), :], qt)
        if rel >= 0:
            s = jnp.where(rel * tk + krow <= qcol, s, NEG_INF)
        else:
            s = jnp.where(jj >= 0, s, NEG_INF)
            s = jnp.where(rel * tk + krow > qcol - WINDOW, s, NEG_INF)
        st = _softmax_step_t(s, vwt_ref[jc], *st)
    o_win = st[2] / st[1]

    gates_t = _sigmoid(misc_ref[...]).T
    for h in range(nh):
        cols = slice(h * tq, (h + 1) * tq)
        c0 = MISC_BRANCH + 3 * h
        o = (gates_t[c0:c0 + 1, :] * o_cmp[:, cols] + gates_t[c0 + 1:c0 + 2, :] * o_slc[:, cols]
             + gates_t[c0 + 2:c0 + 3, :] * o_win[:, cols])
        o_ref[:, _head(h)] = (o.T * _silu(g_ref[:, _head(h)])).astype(o_ref.dtype)


def nsa_attention(p, kc, vc, cos_t, sin_t, batch, seq, *, tq=512, tk=256):
    nq = seq // tq
    nk = seq // tk
    hb = HEAD_DIM
    n_cmp = (seq - CMP_LEN) // CMP_STRIDE + 1
    n_sel = seq // SEL_LEN
    ncp = kc.shape[1]
    assert ncp == LANES and n_cmp <= ncp and n_sel <= LANES and n_sel % 16 == 0
    assert SEL_LEN == 64 and WINDOW % tk == 0 and WINDOW <= tq and tq % tk == 0 and tk % LANES == 0
    assert tq & (tq - 1) == 0
    cmp_start = np.arange(n_cmp) * CMP_STRIDE
    sel_start = np.arange(n_sel) * SEL_LEN
    overlap = np.clip(np.minimum(cmp_start[:, None] + CMP_LEN, sel_start[None, :] + SEL_LEN)
                      - np.maximum(cmp_start[:, None], sel_start[None, :]), 0, None)
    c2s = np.zeros((n_sel, ncp), np.float32)
    c2s[:, :n_cmp] = (overlap / CMP_LEN).T
    expand = np.zeros((nk, tk, LANES), np.float32)
    keys = np.arange(seq)
    expand[keys // tk, keys % tk, keys // SEL_LEN] = 1.0
    kv = lambda c: pl.BlockSpec((seq, hb), lambda b, i: (b, c // hb))
    return pl.pallas_call(
        functools.partial(_nsa_kernel, tq=tq, tk=tk, scale=HEAD_DIM ** -0.5, n_sel=n_sel),
        grid=(batch, nq),
        in_specs=[
            pl.BlockSpec((tq, GROUP), lambda b, i: (b * nq + i, C_NSA_Q // GROUP)),
            kv(C_NSA_KS), kv(C_NSA_VS), kv(C_NSA_KW), kv(C_NSA_VW),
            pl.BlockSpec((None, ncp, hb), lambda b, i: (b, 0, 0)),
            pl.BlockSpec((None, ncp, hb), lambda b, i: (b, 0, 0)),
            pl.BlockSpec((tq, LANES), lambda b, i: (b * nq + i, C_MISC // LANES)),
            pl.BlockSpec((tq, GROUP), lambda b, i: (b * nq + i, C_NSA_G // GROUP)),
            pl.BlockSpec((tq, hb), lambda b, i: (i, 0)),
            pl.BlockSpec((tq, hb), lambda b, i: (i, 0)),
            pl.BlockSpec((seq, hb), lambda b, i: (0, 0)),
            pl.BlockSpec((seq, hb), lambda b, i: (0, 0)),
            pl.BlockSpec((n_sel, ncp), lambda b, i: (0, 0)),
            pl.BlockSpec((nk, tk, LANES), lambda b, i: (0, 0, 0)),
        ],
        out_specs=pl.BlockSpec((tq, GROUP), lambda b, i: (b * nq + i, 0)),
        out_shape=jax.ShapeDtypeStruct((batch * seq, GROUP), BF16),
        scratch_shapes=[pltpu.VMEM((seq, hb), BF16), pltpu.VMEM((nk, hb, tk), BF16),
                        pltpu.VMEM((seq, hb), BF16), pltpu.VMEM((nk, hb, tk), BF16),
                        pltpu.VMEM((hb, ncp), BF16)],
        compiler_params=_params("parallel", "arbitrary"),
        name="nsa_attention",
    )(p, p, p, p, p, kc, vc, p, p, cos_t, sin_t, cos_t, sin_t,
      jnp.asarray(c2s, BF16), jnp.asarray(expand, BF16))


def _out_kernel(ma_ref, mb_ref, mc_ref, md_ref, w_ref, g_ref, x_ref, o_ref):
    y = _dot(ma_ref[...], w_ref[0])
    y = y + _dot(mb_ref[...], w_ref[1])
    y = y + _dot(mc_ref[...], w_ref[2])
    y = y + _dot(md_ref[...], w_ref[3])
    ms = jnp.mean(y * y, axis=-1, keepdims=True)
    o_ref[...] = x_ref[...] + y * lax.rsqrt(ms + RMS_EPS) * g_ref[...]


def out_projection(mixes, w, g, x, *, tm=256):
    m, d = x.shape
    mix_spec = pl.BlockSpec((tm, GROUP), lambda i: (i, 0))
    return pl.pallas_call(
        _out_kernel,
        grid=(m // tm,),
        in_specs=[mix_spec, mix_spec, mix_spec, mix_spec,
                  pl.BlockSpec((len(mixes), GROUP, d), lambda i: (0, 0, 0)),
                  pl.BlockSpec((1, d), lambda i: (0, 0)),
                  pl.BlockSpec((tm, d), lambda i: (i, 0))],
        out_specs=pl.BlockSpec((tm, d), lambda i: (i, 0)),
        out_shape=jax.ShapeDtypeStruct((m, d), F32),
        compiler_params=_params("parallel"),
        name="out_projection",
    )(*mixes, w, g, x)


def _spread_rope(w):
    z = jnp.zeros(w.shape[:-1] + (MLA_ROPE // 2,), w.dtype)
    return jnp.concatenate([w[..., :MLA_ROPE // 2], z, w[..., MLA_ROPE // 2:], z], axis=-1)


def _pack_in_projection(w):
    def cols(name):
        o, n = _ORIG[name]
        return w[..., o:o + n]

    def zeros(n):
        return jnp.zeros(w.shape[:-1] + (n,), w.dtype)

    misc = jnp.concatenate([cols("fox_f"), zeros(MISC_BRANCH - 4), cols("nsa_branch"),
                            zeros(LANES - MISC_BRANCH - 12)], axis=-1)
    packed = jnp.concatenate([
        cols("sb_q"), cols("sb_k"), cols("sb_v"), cols("sb_gate"),
        cols("nsa_q"), cols("nsa_k_cmp"), cols("nsa_v_cmp"), cols("nsa_k_sel"), cols("nsa_v_sel"),
        cols("nsa_k_win"), cols("nsa_v_win"), misc, cols("mla_ckv"), cols("nsa_gate"),
        cols("fox_q"), cols("fox_k"), cols("fox_v"), cols("fox_gate"),
        cols("mla_cq"), _spread_rope(cols("mla_k_rope")), cols("mla_gate"),
    ], axis=-1)
    assert packed.shape[-1] == P_WIDTH
    return packed


def _packed_segments():
    segs = []

    def put(dst, name, lo=0, n=None):
        o, w = _ORIG[name]
        segs.append((dst, o + lo, w - lo if n is None else n))

    for dst, name in ((C_SB_Q, "sb_q"), (C_SB_K, "sb_k"), (C_SB_V, "sb_v"), (C_SB_G, "sb_gate"),
                      (C_NSA_Q, "nsa_q"), (C_NSA_KC, "nsa_k_cmp"), (C_NSA_VC, "nsa_v_cmp"),
                      (C_NSA_KS, "nsa_k_sel"), (C_NSA_VS, "nsa_v_sel"), (C_NSA_KW, "nsa_k_win"),
                      (C_NSA_VW, "nsa_v_win"), (C_MISC + MISC_FOX, "fox_f"), (C_MISC + MISC_BRANCH, "nsa_branch"),
                      (C_MLA_CKV, "mla_ckv"), (C_NSA_G, "nsa_gate"), (C_FOX_Q, "fox_q"), (C_FOX_K, "fox_k"),
                      (C_FOX_V, "fox_v"), (C_FOX_G, "fox_gate"), (C_MLA_CQ, "mla_cq"), (C_MLA_G, "mla_gate")):
        put(dst, name)
    put(C_MLA_KR, "mla_k_rope", 0, MLA_ROPE // 2)
    put(C_MLA_KR + 64, "mla_k_rope", MLA_ROPE // 2, MLA_ROPE // 2)
    return segs


def _pack_kernel(w_ref, o_ref):
    for tile in (C_MISC, C_MLA_KR):
        o_ref[:, tile:tile + LANES] = jnp.zeros((o_ref.shape[0], LANES), o_ref.dtype)
    for dst, src, n in _packed_segments():
        o_ref[:, dst:dst + n] = w_ref[:, src:src + n].astype(o_ref.dtype)


def pack_in_weights(w_in, *, rows=256):
    d = w_in.shape[0]
    return pl.pallas_call(
        _pack_kernel,
        grid=(d // rows,),
        in_specs=[pl.BlockSpec((rows, IN_WIDTH), lambda i: (i, 0))],
        out_specs=pl.BlockSpec((rows, P_WIDTH), lambda i: (i, 0)),
        out_shape=jax.ShapeDtypeStruct((d, P_WIDTH), BF16),
        compiler_params=_params("parallel"),
        name="pack_in_weights",
    )(w_in)


def _pack_uq(w_uq):
    r = w_uq.shape[0]
    w = w_uq.reshape(r, N_HEADS, MLA_NOPE + MLA_ROPE)
    w = jnp.concatenate([w[..., :MLA_NOPE], _spread_rope(w[..., MLA_NOPE:])], axis=-1)
    return w.reshape(r, N_HEADS * 2 * HEAD_DIM)


def _rope_tables(pos, half, spread):
    inv_freq = ROPE_THETA ** (-jnp.arange(half, dtype=F32) / half)
    ang = pos.astype(F32)[:, None] * inv_freq[None, :]
    cos, sin = jnp.cos(ang), jnp.sin(ang)
    if spread:
        z = jnp.zeros_like(cos)
        return jnp.concatenate([cos, z, cos, z], axis=1), jnp.concatenate([-sin, z, sin, z], axis=1)
    return jnp.concatenate([cos, cos], axis=1), jnp.concatenate([-sin, sin], axis=1)


def _layer(x2, batch, seq, pre_g, post_g, w_in, b_in, w_out, forget_bias,
           pos_k, w1_k, w2_k, pos_v, w1_v, w2_v, q_norm_g, w_uq, kv_norm_g, w_ukv, tables):
    d_model = x2.shape[1]
    cos_t, sin_t, cos_c, sin_c, cos_m, sin_m = tables

    w_p = pack_in_weights(w_in)
    b_p = _pack_in_projection(b_in[None, :])
    p = norm_matmul(x2, 0, d_model, pre_g[None, :], w_p, b_p, tm=1024, tn=512, out_dtype=F32)

    o_sb = sb_attention(p, batch, seq)

    ng = seq // CMP_STRIDE
    width = CMP_STRIDE * HEAD_DIM
    t2k = p[:, C_NSA_KC:C_NSA_KC + HEAD_DIM].reshape(batch, ng, width)
    t2v = p[:, C_NSA_VC:C_NSA_VC + HEAD_DIM].reshape(batch, ng, width)
    kc, vc = nsa_compress(t2k, t2v, pos_k.reshape(2, width), pos_v.reshape(2, width),
                          w1_k.astype(BF16), w1_v.astype(BF16), w2_k.astype(BF16), w2_v.astype(BF16),
                          cos_c, sin_c, batch)
    o_nsa = nsa_attention(p, kc, vc, cos_t, sin_t, batch, seq)

    bias_row = jnp.zeros((1, LANES), F32).at[0, MISC_FOX:MISC_FOX + N_HEADS].set(forget_bias)
    ccol, crow = fox_prep(p, bias_row, batch, seq)
    o_fox = fox_attention(p, ccol, crow, batch, seq)

    qp = norm_matmul(p, C_MLA_CQ // MLA_Q_RANK, MLA_Q_RANK, q_norm_g[None, :], _pack_uq(w_uq).astype(BF16),
                     jnp.zeros((1, N_HEADS * 2 * HEAD_DIM), F32), tm=1024, tn=1024, out_dtype=F32)
    kvp = norm_matmul(p, C_MLA_CKV // MLA_KV_RANK, MLA_KV_RANK, kv_norm_g[None, :], w_ukv.astype(BF16),
                      jnp.zeros((1, N_HEADS * 2 * HEAD_DIM), F32), tm=1024, tn=1024, out_dtype=BF16)
    o_mla = mla_attention(p, qp, kvp, cos_m, sin_m, batch, seq)

    w_o = w_out.astype(BF16).reshape(4, GROUP, d_model)
    return out_projection((o_sb, o_nsa, o_fox, o_mla), w_o, post_g[None, :], x2)


def kernel(x, pre_norm_g, post_norm_g, w_in, b_in, w_out, fox_forget_bias, nsa_cmp_pos_k, nsa_cmp_w1_k, nsa_cmp_w2_k, nsa_cmp_pos_v, nsa_cmp_w1_v, nsa_cmp_w2_v, mla_q_norm_g, mla_w_uq, mla_kv_norm_g, mla_w_ukv):
    batch, seq, d_model = x.shape
    depth = w_in.shape[0]
    pos = jnp.arange(seq)
    cmp_end = jnp.arange(seq // CMP_STRIDE) * CMP_STRIDE + (CMP_LEN - 1)
    tables = (_rope_tables(pos, HEAD_DIM // 2, False) + _rope_tables(cmp_end, HEAD_DIM // 2, False)
              + _rope_tables(pos, MLA_ROPE // 2, True))
    x2 = x.reshape(batch * seq, d_model)
    for l in range(depth):
        x2 = _layer(x2, batch, seq, pre_norm_g[l], post_norm_g[l], w_in[l], b_in[l], w_out[l],
                    fox_forget_bias[l], nsa_cmp_pos_k[l], nsa_cmp_w1_k[l], nsa_cmp_w2_k[l],
                    nsa_cmp_pos_v[l], nsa_cmp_w1_v[l], nsa_cmp_w2_v[l],
                    mla_q_norm_g[l], mla_w_uq[l], mla_kv_norm_g[l], mla_w_ukv[l], tables)
    return x2.reshape(batch, seq, d_model)
```

```python
import functools

import numpy as np
import jax
import jax.numpy as jnp
from jax import lax
from jax.experimental import pallas as pl
from jax.experimental.pallas import tpu as pltpu

F32 = jnp.float32
BF16 = jnp.bfloat16

LANES = 128
HEAD_DIM = 128
N_HEADS = 4
GROUP = N_HEADS * HEAD_DIM
RMS_EPS = 1e-6
NEG_INF = -1e30
ROPE_THETA = 10000.0
LOG2E = 1.4426950408889634

CMP_LEN = 32
CMP_STRIDE = 16
SEL_LEN = 64
SEL_TOPN = 16
WINDOW = 512
FORCED_BONUS = 1e6

MLA_Q_RANK = 384
MLA_KV_RANK = 128
MLA_NOPE = 128
MLA_ROPE = 64

VMEM_LIMIT = 56 * 1024 * 1024

C_SB_Q, C_SB_K, C_SB_V, C_SB_G = 0, 512, 1024, 1536
C_NSA_Q = 2048
C_NSA_KC, C_NSA_VC, C_NSA_KS, C_NSA_VS, C_NSA_KW, C_NSA_VW = 2560, 2688, 2816, 2944, 3072, 3200
C_MISC = 3328
C_MLA_CKV = 3456
C_NSA_G = 3584
C_FOX_Q, C_FOX_K, C_FOX_V, C_FOX_G = 4096, 4608, 5120, 5632
C_MLA_CQ = 6144
C_MLA_KR = 6528
C_MLA_G = 6656
P_WIDTH = 7168
assert all(c % GROUP == 0 for c in (C_SB_Q, C_SB_K, C_SB_V, C_SB_G, C_NSA_Q, C_NSA_G, C_FOX_Q, C_FOX_K,
                                    C_FOX_V, C_FOX_G, C_MLA_G)) and C_MLA_CQ % MLA_Q_RANK == 0
MISC_FOX = 0
MISC_BRANCH = 8

_ORIG = {}
_off = 0
for _name, _w in (("sb_q", 512), ("sb_k", 512), ("sb_v", 512), ("sb_gate", 512),
                  ("nsa_q", 512), ("nsa_k_cmp", 128), ("nsa_v_cmp", 128), ("nsa_k_sel", 128),
                  ("nsa_v_sel", 128), ("nsa_k_win", 128), ("nsa_v_win", 128), ("nsa_branch", 12),
                  ("nsa_gate", 512), ("fox_q", 512), ("fox_k", 512), ("fox_v", 512), ("fox_f", 4),
                  ("fox_gate", 512), ("mla_cq", 384), ("mla_ckv", 128), ("mla_k_rope", 64),
                  ("mla_gate", 512)):
    _ORIG[_name] = (_off, _w)
    _off += _w
IN_WIDTH = _off


def _params(*sem):
    return pltpu.CompilerParams(dimension_semantics=sem, vmem_limit_bytes=VMEM_LIMIT)


def _dot(a, b):
    return jnp.dot(a, b, preferred_element_type=F32)


def _split2(x):
    hi = x.astype(BF16)
    lo = (x - hi.astype(F32)).astype(BF16)
    return hi, lo


def _split3(x):
    hi = x.astype(BF16)
    r = x - hi.astype(F32)
    mid = r.astype(BF16)
    lo = (r - mid.astype(F32)).astype(BF16)
    return hi, mid, lo


def _rope(x, cos, sin_signed):
    return x * cos + pltpu.roll(x, 64, 1) * sin_signed


def _silu(g):
    return g / (1.0 + jnp.exp(-g))


def _sigmoid(g):
    return 1.0 / (1.0 + jnp.exp(-g))


def _head(h):
    return slice(h * HEAD_DIM, (h + 1) * HEAD_DIM)


def _cast_rows(dst_ref, src_ref, chunk):
    def body(c, _):
        rows = pl.ds(pl.multiple_of(c * chunk, chunk), chunk)
        dst_ref[rows, :] = src_ref[rows, :].astype(dst_ref.dtype)
        return 0
    lax.fori_loop(0, src_ref.shape[0] // chunk, body, 0)


def _norm_matmul_kernel(x_ref, g_ref, w_ref, b_ref, o_ref, h_ref, *, chunk):
    tm = x_ref.shape[0]

    @pl.when(pl.program_id(1) == 0)
    def _():
        def body(c, _):
            rows = pl.ds(pl.multiple_of(c * chunk, chunk), chunk)
            x = x_ref[rows, :]
            ms = jnp.mean(x * x, axis=-1, keepdims=True)
            h_ref[rows, :] = (x * lax.rsqrt(ms + RMS_EPS) * g_ref[...]).astype(BF16)
            return 0
        lax.fori_loop(0, tm // chunk, body, 0)

    o_ref[...] = (_dot(h_ref[...], w_ref[...]) + b_ref[...]).astype(o_ref.dtype)


def norm_matmul(x, xcol, kdim, g, w, b, *, tm, tn, out_dtype):
    m = x.shape[0]
    n = w.shape[1]
    chunk = min(tm, 256)
    return pl.pallas_call(
        functools.partial(_norm_matmul_kernel, chunk=chunk),
        grid=(m // tm, n // tn),
        in_specs=[
            pl.BlockSpec((tm, kdim), lambda i, j: (i, xcol)),
            pl.BlockSpec((1, kdim), lambda i, j: (0, 0)),
            pl.BlockSpec((kdim, tn), lambda i, j: (0, j)),
            pl.BlockSpec((1, tn), lambda i, j: (0, j)),
        ],
        out_specs=pl.BlockSpec((tm, tn), lambda i, j: (i, j)),
        out_shape=jax.ShapeDtypeStruct((m, n), out_dtype),
        scratch_shapes=[pltpu.VMEM((tm, kdim), BF16)],
        compiler_params=_params("parallel", "arbitrary"),
        name="norm_matmul",
    )(x, g, w, b)


def _softmax_probs_t(s, m, l):
    m_new = jnp.maximum(m, jnp.max(s, axis=0, keepdims=True))
    alpha = jnp.exp2(m - m_new)
    p = jnp.exp2(s - m_new)
    l = alpha * l + jnp.sum(p, axis=0, keepdims=True)
    return m_new, l, alpha, p.astype(BF16)


def _softmax_step_t(s, vt, m, l, acc):
    m, l, alpha, p = _softmax_probs_t(s, m, l)
    return m, l, alpha * acc + _dot(vt, p)


def _softmax_heads_t(scores, vts, state):
    probs = [_softmax_probs_t(s, st[0], st[1]) for s, st in zip(scores, state)]
    return tuple((m, l, alpha * st[2] + _dot(vt, p)) for (m, l, alpha, p), vt, st in zip(probs, vts, state))


def _init_state_t(n, nh):
    return tuple((jnp.full((1, n), NEG_INF, F32), jnp.zeros((1, n), F32), jnp.zeros((HEAD_DIM, n), F32))
                 for _ in range(nh))


def _transpose_blocks(vt_ref, v_ref, tk):
    def body(c, _):
        x = v_ref[pl.ds(pl.multiple_of(c * tk, tk), tk), :].astype(F32)
        for h in range(x.shape[1] // HEAD_DIM):
            vt_ref[c, _head(h), :] = x[:, _head(h)].T.astype(vt_ref.dtype)
        return 0
    lax.fori_loop(0, v_ref.shape[0] // tk, body, 0)


def _sb_kernel(q_ref, k_ref, v_ref, g_ref, ut_ref, o_ref, kb_ref, vt_ref, *, tq, gk, tk, scale):
    i = pl.program_id(1)
    nh = N_HEADS
    nsub = gk // tk

    @pl.when(i == 0)
    def _():
        _cast_rows(kb_ref, k_ref, 256)
        _transpose_blocks(vt_ref, v_ref, gk)

    ut = ut_ref[...]
    qts = [(q_ref[:, _head(h)] * scale).T.astype(BF16) for h in range(nh)]
    krow = lax.broadcasted_iota(jnp.int32, (tk, tq), 0)
    qcol = lax.broadcasted_iota(jnp.int32, (tk, tq), 1)

    def group(gj, carries, accs, diag):
        masked = diag is not None
        goff = pl.multiple_of(gj * gk, gk)
        pairs = [(h, c) for h in range(nh) for c in reversed(range(nsub))]
        zs = {hc: _dot(kb_ref[pl.ds(goff + hc[1] * tk, tk), _head(hc[0])], qts[hc[0]]) for hc in pairs}
        log_beta, log_keep, log_after, masks = {}, {}, {}, {}
        for hc in pairs:
            z = zs[hc]
            softplus = jnp.maximum(z, 0.0) + jnp.log(1.0 + jnp.exp(-jnp.abs(z)))
            lk = -softplus
            log_beta[hc] = z - softplus
            if masked:
                masks[hc] = (diag * gk + hc[1] * tk + krow) < qcol
                lk = jnp.where(masks[hc], lk, 0.0)
            log_keep[hc] = lk
            hi, lo = _split2(lk)
            r = _dot(ut, jnp.concatenate([hi, lo], axis=1))
            log_after[hc] = r[:, :tq] + r[:, tq:]
        new_carries, ws = [], []
        for h in range(nh):
            carry = carries[h]
            w_h = [None] * nsub
            for c in reversed(range(nsub)):
                w = jnp.exp(log_beta[h, c] + log_after[h, c] + carry)
                if masked:
                    w = jnp.where(masks[h, c], w, 0.0)
                w_h[c] = w.astype(BF16)
                carry = carry + jnp.sum(log_keep[h, c], axis=0, keepdims=True)
            ws.append(jnp.concatenate(w_h, axis=0))
            new_carries.append(carry)
        new_accs = [accs[h] + _dot(vt_ref[gj, _head(h), :], ws[h]) for h in range(nh)]
        return tuple(new_carries), tuple(new_accs)

    carries = tuple(jnp.zeros((1, tq), F32) for _ in range(nh))
    accs = tuple(jnp.zeros((HEAD_DIM, tq), F32) for _ in range(nh))
    ndiag = tq // gk
    for d in reversed(range(ndiag)):
        carries, accs = group(i * ndiag + d, carries, accs, d)
    carries, accs = lax.fori_loop(0, i * ndiag, lambda t, c: group(i * ndiag - 1 - t, c[0], c[1], None),
                                  (carries, accs))
    for h in range(nh):
        o_ref[:, _head(h)] = (accs[h].T * _silu(g_ref[:, _head(h)])).astype(o_ref.dtype)


def sb_attention(p, batch, seq, *, tq=512, gk=256, tk=128):
    nq = seq // tq
    ut = jnp.asarray(np.triu(np.ones((tk, tk), np.float32), 1), BF16)
    return pl.pallas_call(
        functools.partial(_sb_kernel, tq=tq, gk=gk, tk=tk, scale=HEAD_DIM ** -0.5),
        grid=(batch, nq),
        in_specs=[
            pl.BlockSpec((tq, GROUP), lambda b, i: (b * nq + i, C_SB_Q // GROUP)),
            pl.BlockSpec((seq, GROUP), lambda b, i: (b, C_SB_K // GROUP)),
            pl.BlockSpec((seq, GROUP), lambda b, i: (b, C_SB_V // GROUP)),
            pl.BlockSpec((tq, GROUP), lambda b, i: (b * nq + i, C_SB_G // GROUP)),
            pl.BlockSpec((tk, tk), lambda b, i: (0, 0)),
        ],
        out_specs=pl.BlockSpec((tq, GROUP), lambda b, i: (b * nq + i, 0)),
        out_shape=jax.ShapeDtypeStruct((batch * seq, GROUP), BF16),
        scratch_shapes=[pltpu.VMEM((seq, GROUP), BF16), pltpu.VMEM((seq // gk, GROUP, gk), BF16)],
        compiler_params=_params("parallel", "arbitrary"),
        name="sb_attention",
    )(p, p, p, p, ut)


def _fox_prep_kernel(f_ref, bias_ref, col_ref, row_ref, *, chunk):
    seq = f_ref.shape[0]
    r = lax.broadcasted_iota(jnp.int32, (chunk, chunk), 0)
    c = lax.broadcasted_iota(jnp.int32, (chunk, chunk), 1)
    tri = jnp.where(c <= r, 1.0, 0.0).astype(BF16)
    carry = jnp.zeros((1, LANES), F32)
    for n in range(seq // chunk):
        x = f_ref[n * chunk:(n + 1) * chunk, :] + bias_ref[...]
        logf = jnp.minimum(x, 0.0) - jnp.log(1.0 + jnp.exp(-jnp.abs(x)))
        hi, mid, lo = _split3(logf)
        within = _dot(tri, hi) + _dot(tri, mid) + _dot(tri, lo)
        cum = within + carry
        col_ref[n * chunk:(n + 1) * chunk, :] = cum * LOG2E
        row_ref[:, n * chunk:(n + 1) * chunk] = (cum * LOG2E).T[0:8, :]
        carry = cum[chunk - 1:chunk, :]


def fox_prep(p, bias_row, batch, seq):
    chunk = LANES
    return pl.pallas_call(
        functools.partial(_fox_prep_kernel, chunk=chunk),
        grid=(batch,),
        in_specs=[
            pl.BlockSpec((seq, LANES), lambda b: (b, C_MISC // LANES)),
            pl.BlockSpec((1, LANES), lambda b: (0, 0)),
        ],
        out_specs=[
            pl.BlockSpec((seq, LANES), lambda b: (b, 0)),
            pl.BlockSpec((None, 8, seq), lambda b: (b, 0, 0)),
        ],
        out_shape=[jax.ShapeDtypeStruct((batch * seq, LANES), F32),
                   jax.ShapeDtypeStruct((batch, 8, seq), F32)],
        compiler_params=_params("parallel"),
        name="fox_prep",
    )(p, bias_row)


def _fox_kernel(q_ref, k_ref, v_ref, g_ref, ccol_ref, crow_ref, o_ref, kb_ref, vt_ref, *, tq, tk, scale):
    i = pl.program_id(1)
    nh = N_HEADS

    @pl.when(i == 0)
    def _():
        _cast_rows(kb_ref, k_ref, 256)
        _transpose_blocks(vt_ref, v_ref, tk)

    qts = [(q_ref[:, _head(h)] * scale).T.astype(BF16) for h in range(nh)]
    qoff = pl.multiple_of(i * tq, tq)
    cqs = [crow_ref[h:h + 1, pl.ds(qoff, tq)] for h in range(nh)]
    krow = lax.broadcasted_iota(jnp.int32, (tk, tq), 0)
    qcol = lax.broadcasted_iota(jnp.int32, (tk, tq), 1)

    def block(j, state, diag):
        off = pl.multiple_of(j * tk, tk)
        qk = [_dot(kb_ref[pl.ds(off, tk), _head(h)], qts[h]) for h in range(nh)]
        scores = []
        for h in range(nh):
            ck = ccol_ref[pl.ds(off, tk), h:h + 1]
            s = qk[h] + cqs[h] - ck
            if diag is not None:
                s = jnp.where(diag * tk + krow <= qcol, s, NEG_INF)
            scores.append(s)
        return _softmax_heads_t(scores, [vt_ref[j, _head(h), :] for h in range(nh)], state)

    state = _init_state_t(tq, nh)
    ndiag = tq // tk
    for d in range(ndiag):
        state = block(i * ndiag + d, state, d)
    state = lax.fori_loop(0, i * ndiag, lambda j, st: block(j, st, None), state)
    for h in range(nh):
        _, l, acc = state[h]
        o_ref[:, _head(h)] = ((acc / l).T * _silu(g_ref[:, _head(h)])).astype(o_ref.dtype)


def fox_attention(p, ccol, crow, batch, seq, *, tq=512, tk=256):
    nq = seq // tq
    return pl.pallas_call(
        functools.partial(_fox_kernel, tq=tq, tk=tk, scale=HEAD_DIM ** -0.5 * LOG2E),
        grid=(batch, nq),
        in_specs=[
            pl.BlockSpec((tq, GROUP), lambda b, i: (b * nq + i, C_FOX_Q // GROUP)),
            pl.BlockSpec((seq, GROUP), lambda b, i: (b, C_FOX_K // GROUP)),
            pl.BlockSpec((seq, GROUP), lambda b, i: (b, C_FOX_V // GROUP)),
            pl.BlockSpec((tq, GROUP), lambda b, i: (b * nq + i, C_FOX_G // GROUP)),
            pl.BlockSpec((seq, LANES), lambda b, i: (b, 0)),
            pl.BlockSpec((None, 8, seq), lambda b, i: (b, 0, 0)),
        ],
        out_specs=pl.BlockSpec((tq, GROUP), lambda b, i: (b * nq + i, 0)),
        out_shape=jax.ShapeDtypeStruct((batch * seq, GROUP), BF16),
        scratch_shapes=[pltpu.VMEM((seq, GROUP), BF16), pltpu.VMEM((seq // tk, GROUP, tk), BF16)],
        compiler_params=_params("parallel", "arbitrary"),
        name="fox_attention",
    )(p, p, p, p, ccol, crow)


def _mla_kernel(q_ref, kv_ref, kr_ref, g_ref, cosq_ref, sinq_ref, cos_ref, sin_ref,
                o_ref, kb_ref, vt_ref, *, tq, tk, scale):
    i = pl.program_id(1)
    nh = N_HEADS
    hw = 2 * HEAD_DIM
    seq = kv_ref.shape[0]

    @pl.when(i == 0)
    def _():
        def body(c, _):
            rows = pl.ds(pl.multiple_of(c * tk, tk), tk)
            kr = _rope(kr_ref[rows, :], cos_ref[rows, :], sin_ref[rows, :]).astype(BF16)
            for h in range(nh):
                kb_ref[rows, h * hw:h * hw + HEAD_DIM] = kv_ref[rows, h * hw:h * hw + HEAD_DIM]
                kb_ref[rows, h * hw + HEAD_DIM:(h + 1) * hw] = kr
                v = kv_ref[rows, h * hw + HEAD_DIM:(h + 1) * hw].astype(F32)
                vt_ref[c, _head(h), :] = v.T.astype(BF16)
            return 0
        lax.fori_loop(0, seq // tk, body, 0)

    cosq = cosq_ref[...]
    sinq = sinq_ref[...]
    qts = []
    for h in range(nh):
        qn = q_ref[:, h * hw:h * hw + HEAD_DIM] * scale
        qr = _rope(q_ref[:, h * hw + HEAD_DIM:(h + 1) * hw], cosq, sinq) * scale
        qts.append(jnp.concatenate([qn.T, qr.T], axis=0).astype(BF16))
    krow = lax.broadcasted_iota(jnp.int32, (tk, tq), 0)
    qcol = lax.broadcasted_iota(jnp.int32, (tk, tq), 1)

    def block(j, state, diag):
        off = pl.multiple_of(j * tk, tk)
        scores = [_dot(kb_ref[pl.ds(off, tk), h * hw:(h + 1) * hw], qts[h]) for h in range(nh)]
        if diag is not None:
            scores = [jnp.where(diag * tk + krow <= qcol, s, NEG_INF) for s in scores]
        return _softmax_heads_t(scores, [vt_ref[j, _head(h), :] for h in range(nh)], state)

    state = _init_state_t(tq, nh)
    ndiag = tq // tk
    for d in range(ndiag):
        state = block(i * ndiag + d, state, d)
    state = lax.fori_loop(0, i * ndiag, lambda j, st: block(j, st, None), state)
    for h in range(nh):
        _, l, acc = state[h]
        o_ref[:, _head(h)] = ((acc / l).T * _silu(g_ref[:, _head(h)])).astype(o_ref.dtype)


def mla_attention(p, qp, kvp, cos_m, sin_m, batch, seq, *, tq=512, tk=256):
    nq = seq // tq
    hb = HEAD_DIM
    wide = N_HEADS * 2 * hb
    return pl.pallas_call(
        functools.partial(_mla_kernel, tq=tq, tk=tk, scale=(MLA_NOPE + MLA_ROPE) ** -0.5 * LOG2E),
        grid=(batch, nq),
        in_specs=[
            pl.BlockSpec((tq, wide), lambda b, i: (b * nq + i, 0)),
            pl.BlockSpec((seq, wide), lambda b, i: (b, 0)),
            pl.BlockSpec((seq, hb), lambda b, i: (b, C_MLA_KR // hb)),
            pl.BlockSpec((tq, GROUP), lambda b, i: (b * nq + i, C_MLA_G // GROUP)),
            pl.BlockSpec((tq, hb), lambda b, i: (i, 0)),
            pl.BlockSpec((tq, hb), lambda b, i: (i, 0)),
            pl.BlockSpec((seq, hb), lambda b, i: (0, 0)),
            pl.BlockSpec((seq, hb), lambda b, i: (0, 0)),
        ],
        out_specs=pl.BlockSpec((tq, GROUP), lambda b, i: (b * nq + i, 0)),
        out_shape=jax.ShapeDtypeStruct((batch * seq, GROUP), BF16),
        scratch_shapes=[pltpu.VMEM((seq, wide), BF16), pltpu.VMEM((seq // tk, GROUP, tk), BF16)],
        compiler_params=_params("parallel", "arbitrary"),
        name="mla_attention",
    )(qp, kvp, p, p, cos_m, sin_m, cos_m, sin_m)


def _compress_kernel(tk_ref, tv_ref, posk_ref, posv_ref, w1k_ref, w1v_ref, w2k_ref, w2v_ref,
                     cos_ref, sin_ref, kc_ref, vc_ref):
    ng = kc_ref.shape[0]
    half = CMP_STRIDE * HEAD_DIM

    def mlp(t_ref, pos_ref, w1_ref, w2_ref):
        t = jnp.concatenate([t_ref[pl.ds(l, ng, stride=CMP_STRIDE), :] for l in range(CMP_STRIDE)], axis=1)
        w1 = w1_ref[...].astype(BF16)
        a = _dot((t + pos_ref[0:1, :]).astype(BF16), w1[0:half, :])
        bm = _dot((t + pos_ref[1:2, :]).astype(BF16), w1[half:2 * half, :])
        hidden = a + pltpu.roll(bm, bm.shape[0] - 1, 0)
        return _dot(_silu(hidden).astype(BF16), w2_ref[...].astype(BF16))

    kc = mlp(tk_ref, posk_ref, w1k_ref, w2k_ref)
    kc_ref[...] = _rope(kc, cos_ref[...], sin_ref[...]).astype(BF16)
    vc_ref[...] = mlp(tv_ref, posv_ref, w1v_ref, w2v_ref).astype(BF16)


def nsa_compress(p, layer, posk, posv, w1k, w1v, w2k, w2v, cos_c, sin_c, batch, seq):
    ng = seq // CMP_STRIDE
    assert CMP_LEN == 2 * CMP_STRIDE
    full = lambda a: pl.BlockSpec(a.shape, lambda b: (0,) * a.ndim)
    per_layer = lambda a: pl.BlockSpec((None,) + a.shape[1:], lambda b: (layer,) + (0,) * (a.ndim - 1))
    return pl.pallas_call(
        _compress_kernel,
        grid=(batch,),
        in_specs=[
            pl.BlockSpec((seq, HEAD_DIM), lambda b: (b, C_NSA_KC // HEAD_DIM)),
            pl.BlockSpec((seq, HEAD_DIM), lambda b: (b, C_NSA_VC // HEAD_DIM)),
            per_layer(posk), per_layer(posv), per_layer(w1k), per_layer(w1v), per_layer(w2k), per_layer(w2v),
            full(cos_c), full(sin_c),
        ],
        out_specs=[pl.BlockSpec((None, ng, HEAD_DIM), lambda b: (b, 0, 0)),
                   pl.BlockSpec((None, ng, HEAD_DIM), lambda b: (b, 0, 0))],
        out_shape=[jax.ShapeDtypeStruct((batch, ng, HEAD_DIM), BF16),
                   jax.ShapeDtypeStruct((batch, ng, HEAD_DIM), BF16)],
        compiler_params=_params("parallel"),
        name="nsa_compress",
    )(p, p, posk, posv, w1k, w1v, w2k, w2v, cos_c, sin_c)


def _nsa_kernel(q_ref, ks_ref, vs_ref, kw_ref, vw_ref, kc_ref, vc_ref, misc_ref, g_ref,
                cosq_ref, sinq_ref, cos_ref, sin_ref, c2s_ref, et_ref, o_ref,
                ksb_ref, vst_ref, kwb_ref, vwt_ref, vct_ref, *, tq, tk, scale, n_sel):
    i = pl.program_id(1)
    seq = ks_ref.shape[0]
    nh = N_HEADS
    ndiag = tq // tk
    n_all = nh * tq

    @pl.when(i == 0)
    def _():
        vct_ref[...] = vc_ref[...].astype(F32).T.astype(BF16)

        def body(c, _):
            rows = pl.ds(pl.multiple_of(c * tk, tk), tk)
            cos = cos_ref[rows, :]
            sin = sin_ref[rows, :]
            ksb_ref[rows, :] = _rope(ks_ref[rows, :], cos, sin).astype(BF16)
            kwb_ref[rows, :] = _rope(kw_ref[rows, :], cos, sin).astype(BF16)
            vst_ref[c] = vs_ref[rows, :].T.astype(BF16)
            vwt_ref[c] = vw_ref[rows, :].T.astype(BF16)
            return 0
        lax.fori_loop(0, seq // tk, body, 0)

    cosq = cosq_ref[...]
    sinq = sinq_ref[...]
    qt = jnp.concatenate([(_rope(q_ref[:, _head(h)], cosq, sinq) * scale).T.astype(BF16)
                          for h in range(nh)], axis=1)

    def heads(x):
        return jnp.concatenate([x] * nh, axis=1)

    n_row = lax.broadcasted_iota(jnp.int32, (LANES, n_all), 0)
    qpos_c = i * tq + (lax.broadcasted_iota(jnp.int32, (LANES, n_all), 1) & (tq - 1))
    zc = _dot(kc_ref[...], qt)
    mask_c = (n_row * CMP_STRIDE + (CMP_LEN - 1)) <= qpos_c
    mc = jnp.max(jnp.where(mask_c, zc, NEG_INF), axis=0, keepdims=True)
    pc = jnp.where(mask_c, jnp.exp2(zc - mc), 0.0)
    lc = jnp.sum(pc, axis=0, keepdims=True)
    pc = pc / jnp.where(lc > 0.0, lc, 1.0)
    o_cmp = _dot(vct_ref[...], pc.astype(BF16))

    pc_sum = pc[:, 0:tq]
    for h in range(1, nh):
        pc_sum = pc_sum + pc[:, h * tq:(h + 1) * tq]
    hi, lo = _split2(pc_sum)
    imp = _dot(c2s_ref[...], hi) + _dot(c2s_ref[...], lo)
    blk = lax.broadcasted_iota(jnp.int32, (n_sel, tq), 0)
    cur = (i * tq + lax.broadcasted_iota(jnp.int32, (n_sel, tq), 1)) >> 6
    valid = blk <= cur
    forced = (blk == 0) | (blk == cur) | (blk == cur - 1)
    score = jnp.where(valid, jnp.where(forced, FORCED_BONUS, imp), NEG_INF)
    rank = jnp.zeros((n_sel, tq), F32)
    for k in range(n_sel):
        sk = score[k:k + 1, :]
        later = jnp.where(blk > k, 1.0, 0.0)
        rank = rank + jnp.where(sk > score, 1.0, 0.0) + jnp.where(sk == score, later, 0.0)
    sel_t = jnp.where(rank < float(SEL_TOPN), 1.0, 0.0)
    sel_t = jnp.concatenate([sel_t, jnp.zeros((LANES - n_sel, tq), F32)], axis=0).astype(BF16)

    krow = lax.broadcasted_iota(jnp.int32, (tk, n_all), 0)
    qcol = lax.broadcasted_iota(jnp.int32, (tk, n_all), 1) & (tq - 1)

    state0 = _init_state_t(n_all, 1)[0]

    def sel_block(j, st, diag):
        off = pl.multiple_of(j * tk, tk)
        s = _dot(ksb_ref[pl.ds(off, tk), :], qt)
        chosen = heads(_dot(et_ref[j], sel_t)) > 0.5
        s = jnp.where(chosen, s, NEG_INF)
        if diag is not None:
            s = jnp.where(diag * tk + krow <= qcol, s, NEG_INF)
        return _softmax_step_t(s, vst_ref[j], *st)

    st = lax.fori_loop(0, i * ndiag, lambda j, s: sel_block(j, s, None), state0)
    for d in range(ndiag):
        st = sel_block(i * ndiag + d, st, d)
    o_slc = st[2] / st[1]

    st = state0
    for rel in list(range(ndiag)) + list(range(-1, -WINDOW // tk - 1, -1)):
        jj = i * ndiag + rel
        jc = jnp.maximum(jj, 0)
        s = _dot(kwb_ref[pl.ds(pl.multiple_of(jc * tk, tk), tk), :], qt)
        if rel >= 0:
            s = jnp.where(rel * tk + krow <= qcol, s, NEG_INF)
        else:
            s = jnp.where(jj >= 0, s, NEG_INF)
            s = jnp.where(rel * tk + krow > qcol - WINDOW, s, NEG_INF)
        st = _softmax_step_t(s, vwt_ref[jc], *st)
    o_win = st[2] / st[1]

    gates_t = _sigmoid(misc_ref[...]).T
    for h in range(nh):
        cols = slice(h * tq, (h + 1) * tq)
        c0 = MISC_BRANCH + 3 * h
        o = (gates_t[c0:c0 + 1, :] * o_cmp[:, cols] + gates_t[c0 + 1:c0 + 2, :] * o_slc[:, cols]
             + gates_t[c0 + 2:c0 + 3, :] * o_win[:, cols])
        o_ref[:, _head(h)] = (o.T * _silu(g_ref[:, _head(h)])).astype(o_ref.dtype)


def nsa_attention(p, kc, vc, cos_t, sin_t, batch, seq, *, tq=512, tk=256):
    nq = seq // tq
    nk = seq // tk
    hb = HEAD_DIM
    n_cmp = (seq - CMP_LEN) // CMP_STRIDE + 1
    n_sel = seq // SEL_LEN
    ncp = kc.shape[1]
    assert ncp == LANES and n_cmp <= ncp and n_sel <= LANES and n_sel % 16 == 0
    assert SEL_LEN == 64 and WINDOW % tk == 0 and WINDOW <= tq and tq % tk == 0 and tk % LANES == 0
    assert tq & (tq - 1) == 0
    cmp_start = np.arange(n_cmp) * CMP_STRIDE
    sel_start = np.arange(n_sel) * SEL_LEN
    overlap = np.clip(np.minimum(cmp_start[:, None] + CMP_LEN, sel_start[None, :] + SEL_LEN)
                      - np.maximum(cmp_start[:, None], sel_start[None, :]), 0, None)
    c2s = np.zeros((n_sel, ncp), np.float32)
    c2s[:, :n_cmp] = (overlap / CMP_LEN).T
    expand = np.zeros((nk, tk, LANES), np.float32)
    keys = np.arange(seq)
    expand[keys // tk, keys % tk, keys // SEL_LEN] = 1.0
    kv = lambda c: pl.BlockSpec((seq, hb), lambda b, i: (b, c // hb))
    return pl.pallas_call(
        functools.partial(_nsa_kernel, tq=tq, tk=tk, scale=HEAD_DIM ** -0.5 * LOG2E, n_sel=n_sel),
        grid=(batch, nq),
        in_specs=[
            pl.BlockSpec((tq, GROUP), lambda b, i: (b * nq + i, C_NSA_Q // GROUP)),
            kv(C_NSA_KS), kv(C_NSA_VS), kv(C_NSA_KW), kv(C_NSA_VW),
            pl.BlockSpec((None, ncp, hb), lambda b, i: (b, 0, 0)),
            pl.BlockSpec((None, ncp, hb), lambda b, i: (b, 0, 0)),
            pl.BlockSpec((tq, LANES), lambda b, i: (b * nq + i, C_MISC // LANES)),
            pl.BlockSpec((tq, GROUP), lambda b, i: (b * nq + i, C_NSA_G // GROUP)),
            pl.BlockSpec((tq, hb), lambda b, i: (i, 0)),
            pl.BlockSpec((tq, hb), lambda b, i: (i, 0)),
            pl.BlockSpec((seq, hb), lambda b, i: (0, 0)),
            pl.BlockSpec((seq, hb), lambda b, i: (0, 0)),
            pl.BlockSpec((n_sel, ncp), lambda b, i: (0, 0)),
            pl.BlockSpec((nk, tk, LANES), lambda b, i: (0, 0, 0)),
        ],
        out_specs=pl.BlockSpec((tq, GROUP), lambda b, i: (b * nq + i, 0)),
        out_shape=jax.ShapeDtypeStruct((batch * seq, GROUP), BF16),
        scratch_shapes=[pltpu.VMEM((seq, hb), BF16), pltpu.VMEM((nk, hb, tk), BF16),
                        pltpu.VMEM((seq, hb), BF16), pltpu.VMEM((nk, hb, tk), BF16),
                        pltpu.VMEM((hb, ncp), BF16)],
        compiler_params=_params("parallel", "arbitrary"),
        name="nsa_attention",
    )(p, p, p, p, p, kc, vc, p, p, cos_t, sin_t, cos_t, sin_t,
      jnp.asarray(c2s, BF16), jnp.asarray(expand, BF16))


def _out_kernel(ma_ref, mb_ref, mc_ref, md_ref, w_ref, g_ref, x_ref, o_ref):
    y = _dot(ma_ref[...], w_ref[0])
    y = y + _dot(mb_ref[...], w_ref[1])
    y = y + _dot(mc_ref[...], w_ref[2])
    y = y + _dot(md_ref[...], w_ref[3])
    ms = jnp.mean(y * y, axis=-1, keepdims=True)
    o_ref[...] = x_ref[...] + y * lax.rsqrt(ms + RMS_EPS) * g_ref[...]


def out_projection(mixes, w, g, x, *, tm=256):
    m, d = x.shape
    mix_spec = pl.BlockSpec((tm, GROUP), lambda i: (i, 0))
    return pl.pallas_call(
        _out_kernel,
        grid=(m // tm,),
        in_specs=[mix_spec, mix_spec, mix_spec, mix_spec,
                  pl.BlockSpec((len(mixes), GROUP, d), lambda i: (0, 0, 0)),
                  pl.BlockSpec((1, d), lambda i: (0, 0)),
                  pl.BlockSpec((tm, d), lambda i: (i, 0))],
        out_specs=pl.BlockSpec((tm, d), lambda i: (i, 0)),
        out_shape=jax.ShapeDtypeStruct((m, d), F32),
        compiler_params=_params("parallel"),
        name="out_projection",
    )(*mixes, w, g, x)


def _spread_rope(w):
    z = jnp.zeros(w.shape[:-1] + (MLA_ROPE // 2,), w.dtype)
    return jnp.concatenate([w[..., :MLA_ROPE // 2], z, w[..., MLA_ROPE // 2:], z], axis=-1)


def _pack_in_projection(w):
    def cols(name):
        o, n = _ORIG[name]
        return w[..., o:o + n]

    def zeros(n):
        return jnp.zeros(w.shape[:-1] + (n,), w.dtype)

    misc = jnp.concatenate([cols("fox_f"), zeros(MISC_BRANCH - 4), cols("nsa_branch"),
                            zeros(LANES - MISC_BRANCH - 12)], axis=-1)
    packed = jnp.concatenate([
        cols("sb_q"), cols("sb_k"), cols("sb_v"), cols("sb_gate"),
        cols("nsa_q"), cols("nsa_k_cmp"), cols("nsa_v_cmp"), cols("nsa_k_sel"), cols("nsa_v_sel"),
        cols("nsa_k_win"), cols("nsa_v_win"), misc, cols("mla_ckv"), cols("nsa_gate"),
        cols("fox_q"), cols("fox_k"), cols("fox_v"), cols("fox_gate"),
        cols("mla_cq"), _spread_rope(cols("mla_k_rope")), cols("mla_gate"),
    ], axis=-1)
    assert packed.shape[-1] == P_WIDTH
    return packed


def _packed_segments():
    segs = []

    def put(dst, name, lo=0, n=None):
        o, w = _ORIG[name]
        segs.append((dst, o + lo, w - lo if n is None else n))

    for dst, name in ((C_SB_Q, "sb_q"), (C_SB_K, "sb_k"), (C_SB_V, "sb_v"), (C_SB_G, "sb_gate"),
                      (C_NSA_Q, "nsa_q"), (C_NSA_KC, "nsa_k_cmp"), (C_NSA_VC, "nsa_v_cmp"),
                      (C_NSA_KS, "nsa_k_sel"), (C_NSA_VS, "nsa_v_sel"), (C_NSA_KW, "nsa_k_win"),
                      (C_NSA_VW, "nsa_v_win"), (C_MISC + MISC_FOX, "fox_f"), (C_MISC + MISC_BRANCH, "nsa_branch"),
                      (C_MLA_CKV, "mla_ckv"), (C_NSA_G, "nsa_gate"), (C_FOX_Q, "fox_q"), (C_FOX_K, "fox_k"),
                      (C_FOX_V, "fox_v"), (C_FOX_G, "fox_gate"), (C_MLA_CQ, "mla_cq"), (C_MLA_G, "mla_gate")):
        put(dst, name)
    put(C_MLA_KR, "mla_k_rope", 0, MLA_ROPE // 2)
    put(C_MLA_KR + 64, "mla_k_rope", MLA_ROPE // 2, MLA_ROPE // 2)
    return segs


def _pack_kernel(w_ref, o_ref):
    for tile in (C_MISC, C_MLA_KR):
        o_ref[:, tile:tile + LANES] = jnp.zeros((o_ref.shape[0], LANES), o_ref.dtype)
    for dst, src, n in _packed_segments():
        o_ref[:, dst:dst + n] = w_ref[:, src:src + n].astype(o_ref.dtype)


def pack_in_weights(w_in, layer, *, rows=256):
    d = w_in.shape[1]
    return pl.pallas_call(
        _pack_kernel,
        grid=(d // rows,),
        in_specs=[pl.BlockSpec((None, rows, IN_WIDTH), lambda i: (layer, i, 0))],
        out_specs=pl.BlockSpec((rows, P_WIDTH), lambda i: (i, 0)),
        out_shape=jax.ShapeDtypeStruct((d, P_WIDTH), BF16),
        compiler_params=_params("parallel"),
        name="pack_in_weights",
    )(w_in)


def _pack_uq(w_uq):
    r = w_uq.shape[0]
    w = w_uq.reshape(r, N_HEADS, MLA_NOPE + MLA_ROPE)
    w = jnp.concatenate([w[..., :MLA_NOPE], _spread_rope(w[..., MLA_NOPE:])], axis=-1)
    return w.reshape(r, N_HEADS * 2 * HEAD_DIM)


def _rope_tables(pos, half, spread):
    inv_freq = ROPE_THETA ** (-jnp.arange(half, dtype=F32) / half)
    ang = pos.astype(F32)[:, None] * inv_freq[None, :]
    cos, sin = jnp.cos(ang), jnp.sin(ang)
    if spread:
        z = jnp.zeros_like(cos)
        return jnp.concatenate([cos, z, cos, z], axis=1), jnp.concatenate([-sin, z, sin, z], axis=1)
    return jnp.concatenate([cos, cos], axis=1), jnp.concatenate([-sin, sin], axis=1)


def _layer(x2, batch, seq, layer, pre_g, post_g, w_in, b_in, w_out, forget_bias,
           pos_k, w1_k, w2_k, pos_v, w1_v, w2_v, q_norm_g, w_uq, kv_norm_g, w_ukv, tables):
    d_model = x2.shape[1]
    cos_t, sin_t, cos_c, sin_c, cos_m, sin_m = tables

    w_p = pack_in_weights(w_in, layer)
    b_p = _pack_in_projection(b_in[None, :])
    p = norm_matmul(x2, 0, d_model, pre_g[None, :], w_p, b_p, tm=1024, tn=512, out_dtype=F32)

    o_sb = sb_attention(p, batch, seq)

    kc, vc = nsa_compress(p, layer, pos_k, pos_v, w1_k, w1_v, w2_k, w2_v, cos_c, sin_c, batch, seq)
    o_nsa = nsa_attention(p, kc, vc, cos_t, sin_t, batch, seq)

    bias_row = jnp.zeros((1, LANES), F32).at[0, MISC_FOX:MISC_FOX + N_HEADS].set(forget_bias)
    ccol, crow = fox_prep(p, bias_row, batch, seq)
    o_fox = fox_attention(p, ccol, crow, batch, seq)

    qp = norm_matmul(p, C_MLA_CQ // MLA_Q_RANK, MLA_Q_RANK, q_norm_g[None, :], _pack_uq(w_uq).astype(BF16),
                     jnp.zeros((1, N_HEADS * 2 * HEAD_DIM), F32), tm=1024, tn=1024, out_dtype=F32)
    kvp = norm_matmul(p, C_MLA_CKV // MLA_KV_RANK, MLA_KV_RANK, kv_norm_g[None, :], w_ukv.astype(BF16),
                      jnp.zeros((1, N_HEADS * 2 * HEAD_DIM), F32), tm=1024, tn=1024, out_dtype=BF16)
    o_mla = mla_attention(p, qp, kvp, cos_m, sin_m, batch, seq)

    w_o = w_out.astype(BF16).reshape(4, GROUP, d_model)
    return out_projection((o_sb, o_nsa, o_fox, o_mla), w_o, post_g[None, :], x2)


def kernel(x, pre_norm_g, post_norm_g, w_in, b_in, w_out, fox_forget_bias, nsa_cmp_pos_k, nsa_cmp_w1_k, nsa_cmp_w2_k, nsa_cmp_pos_v, nsa_cmp_w1_v, nsa_cmp_w2_v, mla_q_norm_g, mla_w_uq, mla_kv_norm_g, mla_w_ukv):
    batch, seq, d_model = x.shape
    depth = w_in.shape[0]
    pos = jnp.arange(seq)
    cmp_end = jnp.arange(seq // CMP_STRIDE) * CMP_STRIDE + (CMP_LEN - 1)
    tables = (_rope_tables(pos, HEAD_DIM // 2, False) + _rope_tables(cmp_end, HEAD_DIM // 2, False)
              + _rope_tables(pos, MLA_ROPE // 2, True))
    x2 = x.reshape(batch * seq, d_model)
    pos_k = nsa_cmp_pos_k.reshape(depth, 2, CMP_STRIDE * HEAD_DIM)
    pos_v = nsa_cmp_pos_v.reshape(depth, 2, CMP_STRIDE * HEAD_DIM)
    for l in range(depth):
        x2 = _layer(x2, batch, seq, l, pre_norm_g[l], post_norm_g[l], w_in, b_in[l], w_out[l],
                    fox_forget_bias[l], pos_k, nsa_cmp_w1_k, nsa_cmp_w2_k, pos_v, nsa_cmp_w1_v, nsa_cmp_w2_v,
                    mla_q_norm_g[l], mla_w_uq[l], mla_kv_norm_g[l], mla_w_ukv[l], tables)
    return x2.reshape(batch, seq, d_model)
```

```python
import functools

import numpy as np
import jax
import jax.numpy as jnp
from jax import lax
from jax.experimental import pallas as pl
from jax.experimental.pallas import tpu as pltpu

F32 = jnp.float32
BF16 = jnp.bfloat16

LANES = 128
HEAD_DIM = 128
N_HEADS = 4
GROUP = N_HEADS * HEAD_DIM
RMS_EPS = 1e-6
NEG_INF = -1e30
ROPE_THETA = 10000.0
LOG2E = 1.4426950408889634

CMP_LEN = 32
CMP_STRIDE = 16
SEL_LEN = 64
SEL_TOPN = 16
WINDOW = 512
FORCED_BONUS = 1e6

MLA_Q_RANK = 384
MLA_KV_RANK = 128
MLA_NOPE = 128
MLA_ROPE = 64

VMEM_LIMIT = 56 * 1024 * 1024

C_SB_Q, C_SB_K, C_SB_V, C_SB_G = 0, 512, 1024, 1536
C_NSA_Q = 2048
C_NSA_KC, C_NSA_VC, C_NSA_KS, C_NSA_VS, C_NSA_KW, C_NSA_VW = 2560, 2688, 2816, 2944, 3072, 3200
C_MISC = 3328
C_MLA_CKV = 3456
C_NSA_G = 3584
C_FOX_Q, C_FOX_K, C_FOX_V, C_FOX_G = 4096, 4608, 5120, 5632
C_MLA_CQ = 6144
C_MLA_KR = 6528
C_MLA_G = 6656
P_WIDTH = 7168
assert all(c % GROUP == 0 for c in (C_SB_Q, C_SB_K, C_SB_V, C_SB_G, C_NSA_Q, C_NSA_G, C_FOX_Q, C_FOX_K,
                                    C_FOX_V, C_FOX_G, C_MLA_G)) and C_MLA_CQ % MLA_Q_RANK == 0
MISC_FOX = 0
MISC_BRANCH = 8

_ORIG = {}
_off = 0
for _name, _w in (("sb_q", 512), ("sb_k", 512), ("sb_v", 512), ("sb_gate", 512),
                  ("nsa_q", 512), ("nsa_k_cmp", 128), ("nsa_v_cmp", 128), ("nsa_k_sel", 128),
                  ("nsa_v_sel", 128), ("nsa_k_win", 128), ("nsa_v_win", 128), ("nsa_branch", 12),
                  ("nsa_gate", 512), ("fox_q", 512), ("fox_k", 512), ("fox_v", 512), ("fox_f", 4),
                  ("fox_gate", 512), ("mla_cq", 384), ("mla_ckv", 128), ("mla_k_rope", 64),
                  ("mla_gate", 512)):
    _ORIG[_name] = (_off, _w)
    _off += _w
IN_WIDTH = _off


def _params(*sem):
    return pltpu.CompilerParams(dimension_semantics=sem, vmem_limit_bytes=VMEM_LIMIT)


def _dot(a, b):
    return jnp.dot(a, b, preferred_element_type=F32)


def _split2(x):
    hi = x.astype(BF16)
    lo = (x - hi.astype(F32)).astype(BF16)
    return hi, lo


def _split3(x):
    hi = x.astype(BF16)
    r = x - hi.astype(F32)
    mid = r.astype(BF16)
    lo = (r - mid.astype(F32)).astype(BF16)
    return hi, mid, lo


def _rope(x, cos, sin_signed):
    return x * cos + pltpu.roll(x, 64, 1) * sin_signed


def _silu(g):
    return g / (1.0 + jnp.exp(-g))


def _sigmoid(g):
    return 1.0 / (1.0 + jnp.exp(-g))


def _head(h):
    return slice(h * HEAD_DIM, (h + 1) * HEAD_DIM)


def _cast_rows(dst_ref, src_ref, chunk):
    def body(c, _):
        rows = pl.ds(pl.multiple_of(c * chunk, chunk), chunk)
        dst_ref[rows, :] = src_ref[rows, :].astype(dst_ref.dtype)
        return 0
    lax.fori_loop(0, src_ref.shape[0] // chunk, body, 0)


def _norm_matmul_kernel(x_ref, g_ref, w_ref, b_ref, o_ref, h_ref, *, chunk):
    tm = x_ref.shape[0]

    @pl.when(pl.program_id(1) == 0)
    def _():
        def body(c, _):
            rows = pl.ds(pl.multiple_of(c * chunk, chunk), chunk)
            x = x_ref[rows, :]
            ms = jnp.mean(x * x, axis=-1, keepdims=True)
            h_ref[rows, :] = (x * lax.rsqrt(ms + RMS_EPS) * g_ref[...]).astype(BF16)
            return 0
        lax.fori_loop(0, tm // chunk, body, 0)

    o_ref[...] = (_dot(h_ref[...], w_ref[...]) + b_ref[...]).astype(o_ref.dtype)


def norm_matmul(x, xcol, kdim, g, w, b, *, tm, tn, out_dtype):
    m = x.shape[0]
    n = w.shape[1]
    chunk = min(tm, 256)
    return pl.pallas_call(
        functools.partial(_norm_matmul_kernel, chunk=chunk),
        grid=(m // tm, n // tn),
        in_specs=[
            pl.BlockSpec((tm, kdim), lambda i, j: (i, xcol)),
            pl.BlockSpec((1, kdim), lambda i, j: (0, 0)),
            pl.BlockSpec((kdim, tn), lambda i, j: (0, j)),
            pl.BlockSpec((1, tn), lambda i, j: (0, j)),
        ],
        out_specs=pl.BlockSpec((tm, tn), lambda i, j: (i, j)),
        out_shape=jax.ShapeDtypeStruct((m, n), out_dtype),
        scratch_shapes=[pltpu.VMEM((tm, kdim), BF16)],
        compiler_params=_params("parallel", "arbitrary"),
        name="norm_matmul",
    )(x, g, w, b)


def _softmax_probs_t(s, m, l):
    m_new = jnp.maximum(m, jnp.max(s, axis=0, keepdims=True))
    alpha = jnp.exp2(m - m_new)
    p = jnp.exp2(s - m_new)
    l = alpha * l + jnp.sum(p, axis=0, keepdims=True)
    return m_new, l, alpha, p.astype(BF16)


def _softmax_step_t(s, vt, m, l, acc):
    m, l, alpha, p = _softmax_probs_t(s, m, l)
    return m, l, alpha * acc + _dot(vt, p)


def _softmax_heads_t(scores, vts, state):
    probs = [_softmax_probs_t(s, st[0], st[1]) for s, st in zip(scores, state)]
    return tuple((m, l, alpha * st[2] + _dot(vt, p)) for (m, l, alpha, p), vt, st in zip(probs, vts, state))


def _init_state_t(n, nh):
    return tuple((jnp.full((1, n), NEG_INF, F32), jnp.zeros((1, n), F32), jnp.zeros((HEAD_DIM, n), F32))
                 for _ in range(nh))


def _transpose_blocks(vt_ref, v_ref, tk):
    def body(c, _):
        x = v_ref[pl.ds(pl.multiple_of(c * tk, tk), tk), :].astype(F32)
        for h in range(x.shape[1] // HEAD_DIM):
            vt_ref[c, _head(h), :] = x[:, _head(h)].T.astype(vt_ref.dtype)
        return 0
    lax.fori_loop(0, v_ref.shape[0] // tk, body, 0)


def _sb_kernel(q_ref, k_ref, v_ref, g_ref, ut_ref, o_ref, kb_ref, vt_ref, *, tq, gk, tk, scale):
    i = pl.program_id(1)
    nh = N_HEADS
    nsub = gk // tk

    @pl.when(i == 0)
    def _():
        _cast_rows(kb_ref, k_ref, 256)
        _transpose_blocks(vt_ref, v_ref, gk)

    ut = ut_ref[...]
    qts = [(q_ref[:, _head(h)] * scale).T.astype(BF16) for h in range(nh)]
    krow = lax.broadcasted_iota(jnp.int32, (tk, tq), 0)
    qcol = lax.broadcasted_iota(jnp.int32, (tk, tq), 1)

    def group(gj, carries, accs, diag):
        masked = diag is not None
        goff = pl.multiple_of(gj * gk, gk)
        pairs = [(h, c) for h in range(nh) for c in reversed(range(nsub))]
        zs = {hc: _dot(kb_ref[pl.ds(goff + hc[1] * tk, tk), _head(hc[0])], qts[hc[0]]) for hc in pairs}
        log_beta, log_keep, log_after, masks = {}, {}, {}, {}
        for hc in pairs:
            z = zs[hc]
            softplus = jnp.maximum(z, 0.0) + jnp.log(1.0 + jnp.exp(-jnp.abs(z)))
            lk = -softplus
            log_beta[hc] = z - softplus
            if masked:
                masks[hc] = (diag * gk + hc[1] * tk + krow) < qcol
                lk = jnp.where(masks[hc], lk, 0.0)
            log_keep[hc] = lk
            hi, lo = _split2(lk)
            r = _dot(ut, jnp.concatenate([hi, lo], axis=1))
            log_after[hc] = r[:, :tq] + r[:, tq:]
        new_carries, ws = [], []
        for h in range(nh):
            carry = carries[h]
            w_h = [None] * nsub
            for c in reversed(range(nsub)):
                w = jnp.exp(log_beta[h, c] + log_after[h, c] + carry)
                if masked:
                    w = jnp.where(masks[h, c], w, 0.0)
                w_h[c] = w.astype(BF16)
                carry = carry + jnp.sum(log_keep[h, c], axis=0, keepdims=True)
            ws.append(jnp.concatenate(w_h, axis=0))
            new_carries.append(carry)
        new_accs = [accs[h] + _dot(vt_ref[gj, _head(h), :], ws[h]) for h in range(nh)]
        return tuple(new_carries), tuple(new_accs)

    carries = tuple(jnp.zeros((1, tq), F32) for _ in range(nh))
    accs = tuple(jnp.zeros((HEAD_DIM, tq), F32) for _ in range(nh))
    ndiag = tq // gk
    for d in reversed(range(ndiag)):
        carries, accs = group(i * ndiag + d, carries, accs, d)
    carries, accs = lax.fori_loop(0, i * ndiag, lambda t, c: group(i * ndiag - 1 - t, c[0], c[1], None),
                                  (carries, accs))
    for h in range(nh):
        o_ref[:, _head(h)] = (accs[h].T * _silu(g_ref[:, _head(h)])).astype(o_ref.dtype)


def sb_attention(p, batch, seq, *, tq=512, gk=256, tk=128):
    nq = seq // tq
    ut = jnp.asarray(np.triu(np.ones((tk, tk), np.float32), 1), BF16)
    return pl.pallas_call(
        functools.partial(_sb_kernel, tq=tq, gk=gk, tk=tk, scale=HEAD_DIM ** -0.5),
        grid=(batch, nq),
        in_specs=[
            pl.BlockSpec((tq, GROUP), lambda b, i: (b * nq + i, C_SB_Q // GROUP)),
            pl.BlockSpec((seq, GROUP), lambda b, i: (b, C_SB_K // GROUP)),
            pl.BlockSpec((seq, GROUP), lambda b, i: (b, C_SB_V // GROUP)),
            pl.BlockSpec((tq, GROUP), lambda b, i: (b * nq + i, C_SB_G // GROUP)),
            pl.BlockSpec((tk, tk), lambda b, i: (0, 0)),
        ],
        out_specs=pl.BlockSpec((tq, GROUP), lambda b, i: (b * nq + i, 0)),
        out_shape=jax.ShapeDtypeStruct((batch * seq, GROUP), BF16),
        scratch_shapes=[pltpu.VMEM((seq, GROUP), BF16), pltpu.VMEM((seq // gk, GROUP, gk), BF16)],
        compiler_params=_params("parallel", "arbitrary"),
        name="sb_attention",
    )(p, p, p, p, ut)


def _fox_prep_kernel(f_ref, bias_ref, col_ref, row_ref, *, chunk):
    seq = f_ref.shape[0]
    r = lax.broadcasted_iota(jnp.int32, (chunk, chunk), 0)
    c = lax.broadcasted_iota(jnp.int32, (chunk, chunk), 1)
    tri = jnp.where(c <= r, 1.0, 0.0).astype(BF16)
    carry = jnp.zeros((1, LANES), F32)
    for n in range(seq // chunk):
        x = f_ref[n * chunk:(n + 1) * chunk, :] + bias_ref[...]
        logf = jnp.minimum(x, 0.0) - jnp.log(1.0 + jnp.exp(-jnp.abs(x)))
        hi, mid, lo = _split3(logf)
        within = _dot(tri, hi) + _dot(tri, mid) + _dot(tri, lo)
        cum = within + carry
        col_ref[n * chunk:(n + 1) * chunk, :] = cum * LOG2E
        row_ref[:, n * chunk:(n + 1) * chunk] = (cum * LOG2E).T[0:8, :]
        carry = cum[chunk - 1:chunk, :]


def fox_prep(p, bias_row, batch, seq):
    chunk = LANES
    return pl.pallas_call(
        functools.partial(_fox_prep_kernel, chunk=chunk),
        grid=(batch,),
        in_specs=[
            pl.BlockSpec((seq, LANES), lambda b: (b, C_MISC // LANES)),
            pl.BlockSpec((1, LANES), lambda b: (0, 0)),
        ],
        out_specs=[
            pl.BlockSpec((seq, LANES), lambda b: (b, 0)),
            pl.BlockSpec((None, 8, seq), lambda b: (b, 0, 0)),
        ],
        out_shape=[jax.ShapeDtypeStruct((batch * seq, LANES), F32),
                   jax.ShapeDtypeStruct((batch, 8, seq), F32)],
        compiler_params=_params("parallel"),
        name="fox_prep",
    )(p, bias_row)


def _fox_kernel(q_ref, k_ref, v_ref, g_ref, ccol_ref, crow_ref, o_ref, kb_ref, vt_ref, *, tq, tk, scale):
    i = pl.program_id(1)
    nh = N_HEADS

    @pl.when(i == 0)
    def _():
        _cast_rows(kb_ref, k_ref, 256)
        _transpose_blocks(vt_ref, v_ref, tk)

    qts = [(q_ref[:, _head(h)] * scale).T.astype(BF16) for h in range(nh)]
    qoff = pl.multiple_of(i * tq, tq)
    cqs = [crow_ref[h:h + 1, pl.ds(qoff, tq)] for h in range(nh)]
    krow = lax.broadcasted_iota(jnp.int32, (tk, tq), 0)
    qcol = lax.broadcasted_iota(jnp.int32, (tk, tq), 1)

    def block(j, state, diag):
        off = pl.multiple_of(j * tk, tk)
        qk = [_dot(kb_ref[pl.ds(off, tk), _head(h)], qts[h]) for h in range(nh)]
        scores = []
        for h in range(nh):
            ck = ccol_ref[pl.ds(off, tk), h:h + 1]
            s = qk[h] + cqs[h] - ck
            if diag is not None:
                s = jnp.where(diag * tk + krow <= qcol, s, NEG_INF)
            scores.append(s)
        return _softmax_heads_t(scores, [vt_ref[j, _head(h), :] for h in range(nh)], state)

    state = _init_state_t(tq, nh)
    ndiag = tq // tk
    for d in range(ndiag):
        state = block(i * ndiag + d, state, d)
    state = lax.fori_loop(0, i * ndiag, lambda j, st: block(j, st, None), state)
    for h in range(nh):
        _, l, acc = state[h]
        o_ref[:, _head(h)] = ((acc / l).T * _silu(g_ref[:, _head(h)])).astype(o_ref.dtype)


def fox_attention(p, ccol, crow, batch, seq, *, tq=512, tk=256):
    nq = seq // tq
    return pl.pallas_call(
        functools.partial(_fox_kernel, tq=tq, tk=tk, scale=HEAD_DIM ** -0.5 * LOG2E),
        grid=(batch, nq),
        in_specs=[
            pl.BlockSpec((tq, GROUP), lambda b, i: (b * nq + i, C_FOX_Q // GROUP)),
            pl.BlockSpec((seq, GROUP), lambda b, i: (b, C_FOX_K // GROUP)),
            pl.BlockSpec((seq, GROUP), lambda b, i: (b, C_FOX_V // GROUP)),
            pl.BlockSpec((tq, GROUP), lambda b, i: (b * nq + i, C_FOX_G // GROUP)),
            pl.BlockSpec((seq, LANES), lambda b, i: (b, 0)),
            pl.BlockSpec((None, 8, seq), lambda b, i: (b, 0, 0)),
        ],
        out_specs=pl.BlockSpec((tq, GROUP), lambda b, i: (b * nq + i, 0)),
        out_shape=jax.ShapeDtypeStruct((batch * seq, GROUP), BF16),
        scratch_shapes=[pltpu.VMEM((seq, GROUP), BF16), pltpu.VMEM((seq // tk, GROUP, tk), BF16)],
        compiler_params=_params("parallel", "arbitrary"),
        name="fox_attention",
    )(p, p, p, p, ccol, crow)


def _mla_kernel(q_ref, kv_ref, kr_ref, g_ref, cosq_ref, sinq_ref, cos_ref, sin_ref,
                o_ref, kb_ref, vt_ref, *, tq, tk, scale):
    i = pl.program_id(1)
    nh = N_HEADS
    hw = 2 * HEAD_DIM
    seq = kv_ref.shape[0]

    @pl.when(i == 0)
    def _():
        def body(c, _):
            rows = pl.ds(pl.multiple_of(c * tk, tk), tk)
            kr = _rope(kr_ref[rows, :], cos_ref[rows, :], sin_ref[rows, :]).astype(BF16)
            for h in range(nh):
                kb_ref[rows, h * hw:h * hw + HEAD_DIM] = kv_ref[rows, h * hw:h * hw + HEAD_DIM]
                kb_ref[rows, h * hw + HEAD_DIM:(h + 1) * hw] = kr
                v = kv_ref[rows, h * hw + HEAD_DIM:(h + 1) * hw].astype(F32)
                vt_ref[c, _head(h), :] = v.T.astype(BF16)
            return 0
        lax.fori_loop(0, seq // tk, body, 0)

    cosq = cosq_ref[...]
    sinq = sinq_ref[...]
    qts = []
    for h in range(nh):
        qn = q_ref[:, h * hw:h * hw + HEAD_DIM] * scale
        qr = _rope(q_ref[:, h * hw + HEAD_DIM:(h + 1) * hw], cosq, sinq) * scale
        qts.append(jnp.concatenate([qn.T, qr.T], axis=0).astype(BF16))
    krow = lax.broadcasted_iota(jnp.int32, (tk, tq), 0)
    qcol = lax.broadcasted_iota(jnp.int32, (tk, tq), 1)

    def block(j, state, diag):
        off = pl.multiple_of(j * tk, tk)
        scores = [_dot(kb_ref[pl.ds(off, tk), h * hw:(h + 1) * hw], qts[h]) for h in range(nh)]
        if diag is not None:
            scores = [jnp.where(diag * tk + krow <= qcol, s, NEG_INF) for s in scores]
        return _softmax_heads_t(scores, [vt_ref[j, _head(h), :] for h in range(nh)], state)

    state = _init_state_t(tq, nh)
    ndiag = tq // tk
    for d in range(ndiag):
        state = block(i * ndiag + d, state, d)
    state = lax.fori_loop(0, i * ndiag, lambda j, st: block(j, st, None), state)
    for h in range(nh):
        _, l, acc = state[h]
        o_ref[:, _head(h)] = ((acc / l).T * _silu(g_ref[:, _head(h)])).astype(o_ref.dtype)


def mla_attention(p, qp, kvp, cos_m, sin_m, batch, seq, *, tq=512, tk=256):
    nq = seq // tq
    hb = HEAD_DIM
    wide = N_HEADS * 2 * hb
    return pl.pallas_call(
        functools.partial(_mla_kernel, tq=tq, tk=tk, scale=(MLA_NOPE + MLA_ROPE) ** -0.5 * LOG2E),
        grid=(batch, nq),
        in_specs=[
            pl.BlockSpec((tq, wide), lambda b, i: (b * nq + i, 0)),
            pl.BlockSpec((seq, wide), lambda b, i: (b, 0)),
            pl.BlockSpec((seq, hb), lambda b, i: (b, C_MLA_KR // hb)),
            pl.BlockSpec((tq, GROUP), lambda b, i: (b * nq + i, C_MLA_G // GROUP)),
            pl.BlockSpec((tq, hb), lambda b, i: (i, 0)),
            pl.BlockSpec((tq, hb), lambda b, i: (i, 0)),
            pl.BlockSpec((seq, hb), lambda b, i: (0, 0)),
            pl.BlockSpec((seq, hb), lambda b, i: (0, 0)),
        ],
        out_specs=pl.BlockSpec((tq, GROUP), lambda b, i: (b * nq + i, 0)),
        out_shape=jax.ShapeDtypeStruct((batch * seq, GROUP), BF16),
        scratch_shapes=[pltpu.VMEM((seq, wide), BF16), pltpu.VMEM((seq // tk, GROUP, tk), BF16)],
        compiler_params=_params("parallel", "arbitrary"),
        name="mla_attention",
    )(qp, kvp, p, p, cos_m, sin_m, cos_m, sin_m)


def _compress_kernel(tk_ref, tv_ref, posk_ref, posv_ref, w1k_ref, w1v_ref, w2k_ref, w2v_ref,
                     cos_ref, sin_ref, kc_ref, vc_ref):
    ng = kc_ref.shape[0]
    half = CMP_STRIDE * HEAD_DIM

    def mlp(t_ref, pos_ref, w1_ref, w2_ref):
        t = jnp.concatenate([t_ref[pl.ds(l, ng, stride=CMP_STRIDE), :] for l in range(CMP_STRIDE)], axis=1)
        w1 = w1_ref[...].astype(BF16)
        a = _dot((t + pos_ref[0:1, :]).astype(BF16), w1[0:half, :])
        bm = _dot((t + pos_ref[1:2, :]).astype(BF16), w1[half:2 * half, :])
        hidden = a + pltpu.roll(bm, bm.shape[0] - 1, 0)
        return _dot(_silu(hidden).astype(BF16), w2_ref[...].astype(BF16))

    kc = mlp(tk_ref, posk_ref, w1k_ref, w2k_ref)
    kc_ref[...] = _rope(kc, cos_ref[...], sin_ref[...]).astype(BF16)
    vc_ref[...] = mlp(tv_ref, posv_ref, w1v_ref, w2v_ref).astype(BF16)


def nsa_compress(p, layer, posk, posv, w1k, w1v, w2k, w2v, cos_c, sin_c, batch, seq):
    ng = seq // CMP_STRIDE
    assert CMP_LEN == 2 * CMP_STRIDE
    full = lambda a: pl.BlockSpec(a.shape, lambda b: (0,) * a.ndim)
    per_layer = lambda a: pl.BlockSpec((None,) + a.shape[1:], lambda b: (layer,) + (0,) * (a.ndim - 1))
    return pl.pallas_call(
        _compress_kernel,
        grid=(batch,),
        in_specs=[
            pl.BlockSpec((seq, HEAD_DIM), lambda b: (b, C_NSA_KC // HEAD_DIM)),
            pl.BlockSpec((seq, HEAD_DIM), lambda b: (b, C_NSA_VC // HEAD_DIM)),
            per_layer(posk), per_layer(posv), per_layer(w1k), per_layer(w1v), per_layer(w2k), per_layer(w2v),
            full(cos_c), full(sin_c),
        ],
        out_specs=[pl.BlockSpec((None, ng, HEAD_DIM), lambda b: (b, 0, 0)),
                   pl.BlockSpec((None, ng, HEAD_DIM), lambda b: (b, 0, 0))],
        out_shape=[jax.ShapeDtypeStruct((batch, ng, HEAD_DIM), BF16),
                   jax.ShapeDtypeStruct((batch, ng, HEAD_DIM), BF16)],
        compiler_params=_params("parallel"),
        name="nsa_compress",
    )(p, p, posk, posv, w1k, w1v, w2k, w2v, cos_c, sin_c)


def _nsa_kernel(q_ref, ks_ref, vs_ref, kw_ref, vw_ref, kc_ref, vc_ref, misc_ref, g_ref,
                cosq_ref, sinq_ref, cos_ref, sin_ref, c2s_ref, et_ref, o_ref,
                ksb_ref, vst_ref, kwb_ref, vwt_ref, vct_ref, *, tq, tk, scale, n_sel):
    i = pl.program_id(1)
    seq = ks_ref.shape[0]
    nh = N_HEADS
    ndiag = tq // tk
    n_all = nh * tq

    @pl.when(i == 0)
    def _():
        vct_ref[...] = vc_ref[...].astype(F32).T.astype(BF16)

        def body(c, _):
            rows = pl.ds(pl.multiple_of(c * tk, tk), tk)
            cos = cos_ref[rows, :]
            sin = sin_ref[rows, :]
            ksb_ref[rows, :] = _rope(ks_ref[rows, :], cos, sin).astype(BF16)
            kwb_ref[rows, :] = _rope(kw_ref[rows, :], cos, sin).astype(BF16)
            vst_ref[c] = vs_ref[rows, :].T.astype(BF16)
            vwt_ref[c] = vw_ref[rows, :].T.astype(BF16)
            return 0
        lax.fori_loop(0, seq // tk, body, 0)

    cosq = cosq_ref[...]
    sinq = sinq_ref[...]
    qt = jnp.concatenate([(_rope(q_ref[:, _head(h)], cosq, sinq) * scale).T.astype(BF16)
                          for h in range(nh)], axis=1)

    def heads(x):
        return jnp.concatenate([x] * nh, axis=1)

    n_row = lax.broadcasted_iota(jnp.int32, (LANES, n_all), 0)
    qpos_c = i * tq + (lax.broadcasted_iota(jnp.int32, (LANES, n_all), 1) & (tq - 1))
    zc = _dot(kc_ref[...], qt)
    mask_c = (n_row * CMP_STRIDE + (CMP_LEN - 1)) <= qpos_c
    mc = jnp.max(jnp.where(mask_c, zc, NEG_INF), axis=0, keepdims=True)
    pc = jnp.where(mask_c, jnp.exp2(zc - mc), 0.0)
    lc = jnp.sum(pc, axis=0, keepdims=True)
    pc = pc / jnp.where(lc > 0.0, lc, 1.0)
    o_cmp = _dot(vct_ref[...], pc.astype(BF16))

    pc_sum = pc[:, 0:tq]
    for h in range(1, nh):
        pc_sum = pc_sum + pc[:, h * tq:(h + 1) * tq]
    hi, lo = _split2(pc_sum)
    imp = _dot(c2s_ref[...], hi) + _dot(c2s_ref[...], lo)
    blk = lax.broadcasted_iota(jnp.int32, (n_sel, tq), 0)
    cur = (i * tq + lax.broadcasted_iota(jnp.int32, (n_sel, tq), 1)) >> 6
    valid = blk <= cur
    forced = (blk == 0) | (blk == cur) | (blk == cur - 1)
    score = jnp.where(valid, jnp.where(forced, FORCED_BONUS, imp), NEG_INF)
    rank = jnp.zeros((n_sel, tq), F32)
    for k in range(n_sel):
        sk = score[k:k + 1, :]
        later = jnp.where(blk > k, 1.0, 0.0)
        rank = rank + jnp.where(sk > score, 1.0, 0.0) + jnp.where(sk == score, later, 0.0)
    sel_t = jnp.where(rank < float(SEL_TOPN), 1.0, 0.0)
    sel_t = jnp.concatenate([sel_t, jnp.zeros((LANES - n_sel, tq), F32)], axis=0).astype(BF16)

    krow = lax.broadcasted_iota(jnp.int32, (tk, n_all), 0)
    qcol = lax.broadcasted_iota(jnp.int32, (tk, n_all), 1) & (tq - 1)

    state0 = _init_state_t(n_all, 1)[0]

    def sel_block(j, st, diag):
        off = pl.multiple_of(j * tk, tk)
        s = _dot(ksb_ref[pl.ds(off, tk), :], qt)
        chosen = heads(_dot(et_ref[j], sel_t)) > 0.5
        s = jnp.where(chosen, s, NEG_INF)
        if diag is not None:
            s = jnp.where(diag * tk + krow <= qcol, s, NEG_INF)
        return _softmax_step_t(s, vst_ref[j], *st)

    st = lax.fori_loop(0, i * ndiag, lambda j, s: sel_block(j, s, None), state0)
    for d in range(ndiag):
        st = sel_block(i * ndiag + d, st, d)
    o_slc = st[2] / st[1]

    st = state0
    for rel in list(range(ndiag)) + list(range(-1, -WINDOW // tk - 1, -1)):
        jj = i * ndiag + rel
        jc = jnp.maximum(jj, 0)
        s = _dot(kwb_ref[pl.ds(pl.multiple_of(jc * tk, tk), tk), :], qt)
        if rel >= 0:
            s = jnp.where(rel * tk + krow <= qcol, s, NEG_INF)
        else:
            s = jnp.where(jj >= 0, s, NEG_INF)
            s = jnp.where(rel * tk + krow > qcol - WINDOW, s, NEG_INF)
        st = _softmax_step_t(s, vwt_ref[jc], *st)
    o_win = st[2] / st[1]

    gates_t = _sigmoid(misc_ref[...]).T
    for h in range(nh):
        cols = slice(h * tq, (h + 1) * tq)
        c0 = MISC_BRANCH + 3 * h
        o = (gates_t[c0:c0 + 1, :] * o_cmp[:, cols] + gates_t[c0 + 1:c0 + 2, :] * o_slc[:, cols]
             + gates_t[c0 + 2:c0 + 3, :] * o_win[:, cols])
        o_ref[:, _head(h)] = (o.T * _silu(g_ref[:, _head(h)])).astype(o_ref.dtype)


def nsa_attention(p, kc, vc, cos_t, sin_t, batch, seq, *, tq=512, tk=256):
    nq = seq // tq
    nk = seq // tk
    hb = HEAD_DIM
    n_cmp = (seq - CMP_LEN) // CMP_STRIDE + 1
    n_sel = seq // SEL_LEN
    ncp = kc.shape[1]
    assert ncp == LANES and n_cmp <= ncp and n_sel <= LANES and n_sel % 16 == 0
    assert SEL_LEN == 64 and WINDOW % tk == 0 and WINDOW <= tq and tq % tk == 0 and tk % LANES == 0
    assert tq & (tq - 1) == 0
    cmp_start = np.arange(n_cmp) * CMP_STRIDE
    sel_start = np.arange(n_sel) * SEL_LEN
    overlap = np.clip(np.minimum(cmp_start[:, None] + CMP_LEN, sel_start[None, :] + SEL_LEN)
                      - np.maximum(cmp_start[:, None], sel_start[None, :]), 0, None)
    c2s = np.zeros((n_sel, ncp), np.float32)
    c2s[:, :n_cmp] = (overlap / CMP_LEN).T
    expand = np.zeros((nk, tk, LANES), np.float32)
    keys = np.arange(seq)
    expand[keys // tk, keys % tk, keys // SEL_LEN] = 1.0
    kv = lambda c: pl.BlockSpec((seq, hb), lambda b, i: (b, c // hb))
    return pl.pallas_call(
        functools.partial(_nsa_kernel, tq=tq, tk=tk, scale=HEAD_DIM ** -0.5 * LOG2E, n_sel=n_sel),
        grid=(batch, nq),
        in_specs=[
            pl.BlockSpec((tq, GROUP), lambda b, i: (b * nq + i, C_NSA_Q // GROUP)),
            kv(C_NSA_KS), kv(C_NSA_VS), kv(C_NSA_KW), kv(C_NSA_VW),
            pl.BlockSpec((None, ncp, hb), lambda b, i: (b, 0, 0)),
            pl.BlockSpec((None, ncp, hb), lambda b, i: (b, 0, 0)),
            pl.BlockSpec((tq, LANES), lambda b, i: (b * nq + i, C_MISC // LANES)),
            pl.BlockSpec((tq, GROUP), lambda b, i: (b * nq + i, C_NSA_G // GROUP)),
            pl.BlockSpec((tq, hb), lambda b, i: (i, 0)),
            pl.BlockSpec((tq, hb), lambda b, i: (i, 0)),
            pl.BlockSpec((seq, hb), lambda b, i: (0, 0)),
            pl.BlockSpec((seq, hb), lambda b, i: (0, 0)),
            pl.BlockSpec((n_sel, ncp), lambda b, i: (0, 0)),
            pl.BlockSpec((nk, tk, LANES), lambda b, i: (0, 0, 0)),
        ],
        out_specs=pl.BlockSpec((tq, GROUP), lambda b, i: (b * nq + i, 0)),
        out_shape=jax.ShapeDtypeStruct((batch * seq, GROUP), BF16),
        scratch_shapes=[pltpu.VMEM((seq, hb), BF16), pltpu.VMEM((nk, hb, tk), BF16),
                        pltpu.VMEM((seq, hb), BF16), pltpu.VMEM((nk, hb, tk), BF16),
                        pltpu.VMEM((hb, ncp), BF16)],
        compiler_params=_params("parallel", "arbitrary"),
        name="nsa_attention",
    )(p, p, p, p, p, kc, vc, p, p, cos_t, sin_t, cos_t, sin_t,
      jnp.asarray(c2s, BF16), jnp.asarray(expand, BF16))


def _out_kernel(ma_ref, mb_ref, mc_ref, md_ref, w_ref, g_ref, x_ref, o_ref):
    y = _dot(ma_ref[...], w_ref[0])
    y = y + _dot(mb_ref[...], w_ref[1])
    y = y + _dot(mc_ref[...], w_ref[2])
    y = y + _dot(md_ref[...], w_ref[3])
    ms = jnp.mean(y * y, axis=-1, keepdims=True)
    o_ref[...] = x_ref[...] + y * lax.rsqrt(ms + RMS_EPS) * g_ref[...]


def out_projection(mixes, w, g, x, *, tm=256):
    m, d = x.shape
    mix_spec = pl.BlockSpec((tm, GROUP), lambda i: (i, 0))
    return pl.pallas_call(
        _out_kernel,
        grid=(m // tm,),
        in_specs=[mix_spec, mix_spec, mix_spec, mix_spec,
                  pl.BlockSpec((len(mixes), GROUP, d), lambda i: (0, 0, 0)),
                  pl.BlockSpec((1, d), lambda i: (0, 0)),
                  pl.BlockSpec((tm, d), lambda i: (i, 0))],
        out_specs=pl.BlockSpec((tm, d), lambda i: (i, 0)),
        out_shape=jax.ShapeDtypeStruct((m, d), F32),
        compiler_params=_params("parallel"),
        name="out_projection",
    )(*mixes, w, g, x)


def _spread_rope(w):
    z = jnp.zeros(w.shape[:-1] + (MLA_ROPE // 2,), w.dtype)
    return jnp.concatenate([w[..., :MLA_ROPE // 2], z, w[..., MLA_ROPE // 2:], z], axis=-1)


def _pack_in_projection(w):
    def cols(name):
        o, n = _ORIG[name]
        return w[..., o:o + n]

    def zeros(n):
        return jnp.zeros(w.shape[:-1] + (n,), w.dtype)

    misc = jnp.concatenate([cols("fox_f"), zeros(MISC_BRANCH - 4), cols("nsa_branch"),
                            zeros(LANES - MISC_BRANCH - 12)], axis=-1)
    packed = jnp.concatenate([
        cols("sb_q"), cols("sb_k"), cols("sb_v"), cols("sb_gate"),
        cols("nsa_q"), cols("nsa_k_cmp"), cols("nsa_v_cmp"), cols("nsa_k_sel"), cols("nsa_v_sel"),
        cols("nsa_k_win"), cols("nsa_v_win"), misc, cols("mla_ckv"), cols("nsa_gate"),
        cols("fox_q"), cols("fox_k"), cols("fox_v"), cols("fox_gate"),
        cols("mla_cq"), _spread_rope(cols("mla_k_rope")), cols("mla_gate"),
    ], axis=-1)
    assert packed.shape[-1] == P_WIDTH
    return packed


def _packed_segments():
    segs = []

    def put(dst, name, lo=0, n=None):
        o, w = _ORIG[name]
        segs.append((dst, o + lo, w - lo if n is None else n))

    for dst, name in ((C_SB_Q, "sb_q"), (C_SB_K, "sb_k"), (C_SB_V, "sb_v"), (C_SB_G, "sb_gate"),
                      (C_NSA_Q, "nsa_q"), (C_NSA_KC, "nsa_k_cmp"), (C_NSA_VC, "nsa_v_cmp"),
                      (C_NSA_KS, "nsa_k_sel"), (C_NSA_VS, "nsa_v_sel"), (C_NSA_KW, "nsa_k_win"),
                      (C_NSA_VW, "nsa_v_win"), (C_MISC + MISC_FOX, "fox_f"), (C_MISC + MISC_BRANCH, "nsa_branch"),
                      (C_MLA_CKV, "mla_ckv"), (C_NSA_G, "nsa_gate"), (C_FOX_Q, "fox_q"), (C_FOX_K, "fox_k"),
                      (C_FOX_V, "fox_v"), (C_FOX_G, "fox_gate"), (C_MLA_CQ, "mla_cq"), (C_MLA_G, "mla_gate")):
        put(dst, name)
    put(C_MLA_KR, "mla_k_rope", 0, MLA_ROPE // 2)
    put(C_MLA_KR + 64, "mla_k_rope", MLA_ROPE // 2, MLA_ROPE // 2)
    return segs


def _pack_kernel(w_ref, o_ref):
    for tile in (C_MISC, C_MLA_KR):
        o_ref[:, tile:tile + LANES] = jnp.zeros((o_ref.shape[0], LANES), o_ref.dtype)
    for dst, src, n in _packed_segments():
        o_ref[:, dst:dst + n] = w_ref[:, src:src + n].astype(o_ref.dtype)


def pack_in_weights(w_in, layer, *, rows=256):
    depth, d, _ = w_in.shape
    w_in = w_in.reshape(depth * d, IN_WIDTH)
    return pl.pallas_call(
        _pack_kernel,
        grid=(d // rows,),
        in_specs=[pl.BlockSpec((rows, IN_WIDTH), lambda i: (layer * (d // rows) + i, 0))],
        out_specs=pl.BlockSpec((rows, P_WIDTH), lambda i: (i, 0)),
        out_shape=jax.ShapeDtypeStruct((d, P_WIDTH), BF16),
        compiler_params=_params("parallel"),
        name="pack_in_weights",
    )(w_in)


def _pack_uq(w_uq):
    r = w_uq.shape[0]
    w = w_uq.reshape(r, N_HEADS, MLA_NOPE + MLA_ROPE)
    w = jnp.concatenate([w[..., :MLA_NOPE], _spread_rope(w[..., MLA_NOPE:])], axis=-1)
    return w.reshape(r, N_HEADS * 2 * HEAD_DIM)


def _rope_tables(pos, half, spread):
    inv_freq = ROPE_THETA ** (-jnp.arange(half, dtype=F32) / half)
    ang = pos.astype(F32)[:, None] * inv_freq[None, :]
    cos, sin = jnp.cos(ang), jnp.sin(ang)
    if spread:
        z = jnp.zeros_like(cos)
        return jnp.concatenate([cos, z, cos, z], axis=1), jnp.concatenate([-sin, z, sin, z], axis=1)
    return jnp.concatenate([cos, cos], axis=1), jnp.concatenate([-sin, sin], axis=1)


def _layer(x2, batch, seq, layer, pre_g, post_g, w_in, b_in, w_out, forget_bias,
           pos_k, w1_k, w2_k, pos_v, w1_v, w2_v, q_norm_g, w_uq, kv_norm_g, w_ukv, tables):
    d_model = x2.shape[1]
    cos_t, sin_t, cos_c, sin_c, cos_m, sin_m = tables

    w_p = pack_in_weights(w_in, layer)
    b_p = _pack_in_projection(b_in[None, :])
    p = norm_matmul(x2, 0, d_model, pre_g[None, :], w_p, b_p, tm=1024, tn=512, out_dtype=F32)

    o_sb = sb_attention(p, batch, seq)

    kc, vc = nsa_compress(p, layer, pos_k, pos_v, w1_k, w1_v, w2_k, w2_v, cos_c, sin_c, batch, seq)
    o_nsa = nsa_attention(p, kc, vc, cos_t, sin_t, batch, seq)

    bias_row = jnp.zeros((1, LANES), F32).at[0, MISC_FOX:MISC_FOX + N_HEADS].set(forget_bias)
    ccol, crow = fox_prep(p, bias_row, batch, seq)
    o_fox = fox_attention(p, ccol, crow, batch, seq)

    qp = norm_matmul(p, C_MLA_CQ // MLA_Q_RANK, MLA_Q_RANK, q_norm_g[None, :], _pack_uq(w_uq).astype(BF16),
                     jnp.zeros((1, N_HEADS * 2 * HEAD_DIM), F32), tm=1024, tn=1024, out_dtype=F32)
    kvp = norm_matmul(p, C_MLA_CKV // MLA_KV_RANK, MLA_KV_RANK, kv_norm_g[None, :], w_ukv.astype(BF16),
                      jnp.zeros((1, N_HEADS * 2 * HEAD_DIM), F32), tm=1024, tn=1024, out_dtype=BF16)
    o_mla = mla_attention(p, qp, kvp, cos_m, sin_m, batch, seq)

    w_o = w_out.astype(BF16).reshape(4, GROUP, d_model)
    return out_projection((o_sb, o_nsa, o_fox, o_mla), w_o, post_g[None, :], x2)


def kernel(x, pre_norm_g, post_norm_g, w_in, b_in, w_out, fox_forget_bias, nsa_cmp_pos_k, nsa_cmp_w1_k, nsa_cmp_w2_k, nsa_cmp_pos_v, nsa_cmp_w1_v, nsa_cmp_w2_v, mla_q_norm_g, mla_w_uq, mla_kv_norm_g, mla_w_ukv):
    batch, seq, d_model = x.shape
    depth = w_in.shape[0]
    pos = jnp.arange(seq)
    cmp_end = jnp.arange(seq // CMP_STRIDE) * CMP_STRIDE + (CMP_LEN - 1)
    tables = (_rope_tables(pos, HEAD_DIM // 2, False) + _rope_tables(cmp_end, HEAD_DIM // 2, False)
              + _rope_tables(pos, MLA_ROPE // 2, True))
    x2 = x.reshape(batch * seq, d_model)
    pos_k = nsa_cmp_pos_k.reshape(depth, 2, CMP_STRIDE * HEAD_DIM)
    pos_v = nsa_cmp_pos_v.reshape(depth, 2, CMP_STRIDE * HEAD_DIM)
    for l in range(depth):
        x2 = _layer(x2, batch, seq, l, pre_norm_g[l], post_norm_g[l], w_in, b_in[l], w_out[l],
                    fox_forget_bias[l], pos_k, nsa_cmp_w1_k, nsa_cmp_w2_k, pos_v, nsa_cmp_w1_v, nsa_cmp_w2_v,
                    mla_q_norm_g[l], mla_w_uq[l], mla_kv_norm_g[l], mla_w_ukv[l], tables)
    return x2.reshape(batch, seq, d_model)
```
